```python
import jax, jax.numpy as jnp
from jax import lax
import numpy as np

D_MODEL = 1024
BATCH = 8
SEQ = 4096
DEPTH = 4

GRID_W = 64
PLE_DIM = 256
EPS = 1e-6
GLA_HEADS = 4
GLA_DK = 128
GLA_DV = 256
GLA_KEY = GLA_HEADS * GLA_DK
GLA_VAL = GLA_HEADS * GLA_DV
GLA_RANK = 16
GLA_TAU = 16.0
GLA_CHUNK = 64
ATTN_HEADS = 8
ATTN_KV_HEADS = 2
HEAD_DIM = 128
ATTN_Q = ATTN_HEADS * HEAD_DIM
ATTN_KV = ATTN_KV_HEADS * HEAD_DIM
Q_BLOCK = 128
ROPE_THETA = 10000.0
ROPE_AXIS_DIM = HEAD_DIM // 2
FFN_HIDDEN = -(-8 * D_MODEL // (3 * 256)) * 256
IN_SPLITS = (GLA_KEY, GLA_KEY, GLA_VAL, GLA_VAL, GLA_RANK, GLA_RANK,
             ATTN_Q, ATTN_KV, ATTN_KV, D_MODEL, D_MODEL)
IN_WIDTH = sum(IN_SPLITS)

kernel_name = "hybrid_gla_axialrope_gqa_encoder"


def rms_norm(x, g):
    xf = x.astype(jnp.float32)
    y = xf * lax.rsqrt(jnp.mean(xf * xf, axis=-1, keepdims=True) + EPS)
    return (y * g.astype(jnp.float32)).astype(x.dtype)


def axial_rope_tables(seq_len):
    rows = seq_len // GRID_W
    row = jnp.repeat(jnp.arange(rows, dtype=jnp.float32), GRID_W)
    col = jnp.tile(jnp.arange(GRID_W, dtype=jnp.float32), rows)
    inv = ROPE_THETA ** (-jnp.arange(0, ROPE_AXIS_DIM, 2, dtype=jnp.float32) / ROPE_AXIS_DIM)
    ang = jnp.stack([row[:, None] * inv, col[:, None] * inv], axis=1)
    return jnp.cos(ang), jnp.sin(ang)


def apply_axial_rope(x, cos, sin):
    b, s, h, d = x.shape
    xr = x.astype(jnp.float32).reshape(b, s, h, 2, 2, d // 4)
    c = cos[None, :, None]
    sn = sin[None, :, None]
    x1, x2 = xr[..., 0, :], xr[..., 1, :]
    out = jnp.stack([x1 * c - x2 * sn, x2 * c + x1 * sn], axis=-2)
    return out.reshape(b, s, h, d).astype(x.dtype)


def gla_chunked(q, k, v, log_a):
    b, s, h, dk = q.shape
    dv = v.shape[-1]
    n = s // GLA_CHUNK
    c = GLA_CHUNK

    def chunks(t):
        return t.reshape(b, n, c, h, t.shape[-1]).transpose(1, 0, 3, 2, 4)

    q, k, v, la = chunks(q), chunks(k), chunks(v), chunks(log_a)
    cum = jnp.cumsum(la, axis=3)
    last = cum[:, :, :, -1:, :]
    q_dec = q * jnp.exp(cum)
    k_intra = k * jnp.exp(-cum)
    k_state = k * jnp.exp(last - cum)
    mask = jnp.tril(jnp.ones((c, c), dtype=bool))
    scores = jnp.where(mask, jnp.einsum('nbhid,nbhjd->nbhij', q_dec, k_intra), 0.0)
    o_intra = jnp.einsum('nbhij,nbhjv->nbhiv', scores, v)

    def step(state, xs):
        qd, ks, vc, lst = xs
        o = jnp.einsum('bhid,bhdv->bhiv', qd, state)
        state = state * jnp.exp(lst[:, :, 0, :])[..., None] + jnp.einsum('bhjd,bhjv->bhdv', ks, vc)
        return state, o

    s0 = jnp.zeros((b, h, dk, dv), jnp.float32)
    _, o_inter = lax.scan(step, s0, (q_dec, k_state, v, last))
    o = o_intra + o_inter
    return o.transpose(1, 0, 3, 2, 4).reshape(b, s, h, dv)


def gqa_attention(q, k, v):
    b, s, hq, hd = q.shape
    hkv = k.shape[2]
    g = hq // hkv
    nb = s // Q_BLOCK
    qb = q.reshape(b, nb, Q_BLOCK, hkv, g, hd).transpose(1, 0, 2, 3, 4, 5)
    scale = hd ** -0.5

    def block(qblk):
        sc = jnp.einsum('bqkgd,bskd->bkgqs', qblk, k).astype(jnp.float32) * scale
        pr = jax.nn.softmax(sc, axis=-1).astype(v.dtype)
        return jnp.einsum('bkgqs,bskd->bqkgd', pr, v)

    out = lax.map(block, qb)
    return out.transpose(1, 0, 2, 3, 4, 5).reshape(b, s, hq * hd)


def hybrid_layer(h, p_i, cos, sin, g_mix_pre, w_in, w_alpha_up, b_alpha, g_gla_out,
                 g_q_norm, g_k_norm, w_o_gla, w_o_attn, w_out, g_mix_post,
                 g_ffn_pre, w_ffn_in, w_ffn_out, g_ffn_post, w_ple_proj, w_ple_gate, g_ple_post):
    b, s, _ = h.shape
    f32 = jnp.float32
    u = rms_norm(h, g_mix_pre)
    offsets = np.cumsum(IN_SPLITS)[:-1].tolist()
    (gq, gk, gv, gg, ra_f, ra_b, aq, ak, av, gate_a, gate_b) = jnp.split(u @ w_in, offsets, axis=-1)

    qh = gq.astype(f32).reshape(b, s, GLA_HEADS, GLA_DK) * (GLA_DK ** -0.5)
    kh = gk.astype(f32).reshape(b, s, GLA_HEADS, GLA_DK)
    vh = gv.astype(f32).reshape(b, s, GLA_HEADS, GLA_DV)
    la_f = (jax.nn.log_sigmoid(ra_f.astype(f32) @ w_alpha_up[0].astype(f32) + b_alpha[0].astype(f32))
            / GLA_TAU).reshape(b, s, GLA_HEADS, GLA_DK)
    la_b = (jax.nn.log_sigmoid(ra_b.astype(f32) @ w_alpha_up[1].astype(f32) + b_alpha[1].astype(f32))
            / GLA_TAU).reshape(b, s, GLA_HEADS, GLA_DK)
    o_f = gla_chunked(qh, kh, vh, la_f)
    o_b = gla_chunked(qh[:, ::-1], kh[:, ::-1], vh[:, ::-1], la_b[:, ::-1])[:, ::-1]
    o_gla = rms_norm(o_f + o_b, g_gla_out).reshape(b, s, GLA_VAL).astype(h.dtype)
    branch_a = (o_gla * jax.nn.silu(gg)) @ w_o_gla

    q = apply_axial_rope(rms_norm(aq.reshape(b, s, ATTN_HEADS, HEAD_DIM), g_q_norm), cos, sin)
    k = apply_axial_rope(rms_norm(ak.reshape(b, s, ATTN_KV_HEADS, HEAD_DIM), g_k_norm), cos, sin)
    v = av.reshape(b, s, ATTN_KV_HEADS, HEAD_DIM)
    branch_b = gqa_attention(q, k, v) @ w_o_attn

    mixed = jax.nn.sigmoid(gate_a) * branch_a + jax.nn.sigmoid(gate_b) * branch_b
    h = h + rms_norm(mixed @ w_out, g_mix_post)

    gate, up = jnp.split(rms_norm(h, g_ffn_pre) @ w_ffn_in, 2, axis=-1)
    h = h + rms_norm((jax.nn.silu(gate) * up) @ w_ffn_out, g_ffn_post)

    e = p_i @ w_ple_proj
    h = h + rms_norm(jax.nn.sigmoid(h @ w_ple_gate) * e, g_ple_post)
    return h


def setup_inputs(seed: int = 0) -> dict:
    key = jax.random.key(seed)
    ks = jax.random.split(key, 24)
    f32 = jnp.float32

    def w(k, shape, fan_in):
        return jax.random.normal(k, shape, f32) * (fan_in ** -0.5)

    def gain(k, shape):
        return 1.0 + 0.05 * jax.random.normal(k, shape, f32)

    return {
        "x": jax.random.normal(ks[0], (BATCH, SEQ, D_MODEL), f32),
        "p": jax.random.normal(ks[1], (DEPTH, BATCH, SEQ, PLE_DIM), f32),
        "g_mix_pre": gain(ks[2], (DEPTH, D_MODEL)),
        "w_in": w(ks[3], (DEPTH, D_MODEL, IN_WIDTH), D_MODEL),
        "w_alpha_up": w(ks[4], (DEPTH, 2, GLA_RANK, GLA_KEY), GLA_RANK),
        "b_alpha": 1.0 + 0.1 * jax.random.normal(ks[5], (DEPTH, 2, GLA_KEY), f32),
        "g_gla_out": gain(ks[6], (DEPTH, GLA_DV)),
        "g_q_norm": gain(ks[7], (DEPTH, HEAD_DIM)),
        "g_k_norm": gain(ks[8], (DEPTH, HEAD_DIM)),
        "w_o_gla": w(ks[9], (DEPTH, GLA_VAL, D_MODEL), GLA_VAL),
        "w_o_attn": w(ks[10], (DEPTH, ATTN_Q, D_MODEL), ATTN_Q),
        "w_out": w(ks[11], (DEPTH, D_MODEL, D_MODEL), D_MODEL),
        "g_mix_post": gain(ks[12], (DEPTH, D_MODEL)),
        "g_ffn_pre": gain(ks[13], (DEPTH, D_MODEL)),
        "w_ffn_in": w(ks[14], (DEPTH, D_MODEL, 2 * FFN_HIDDEN), D_MODEL),
        "w_ffn_out": w(ks[15], (DEPTH, FFN_HIDDEN, D_MODEL), FFN_HIDDEN),
        "g_ffn_post": gain(ks[16], (DEPTH, D_MODEL)),
        "w_ple_proj": w(ks[17], (DEPTH, PLE_DIM, D_MODEL), PLE_DIM),
        "w_ple_gate": w(ks[18], (DEPTH, D_MODEL, D_MODEL), D_MODEL),
        "g_ple_post": gain(ks[19], (DEPTH, D_MODEL)),
    }


def reference(x, p, g_mix_pre, w_in, w_alpha_up, b_alpha, g_gla_out, g_q_norm, g_k_norm,
              w_o_gla, w_o_attn, w_out, g_mix_post, g_ffn_pre, w_ffn_in, w_ffn_out,
              g_ffn_post, w_ple_proj, w_ple_gate, g_ple_post):
    cos, sin = axial_rope_tables(x.shape[1])
    h = x
    for i in range(DEPTH):
        h = hybrid_layer(h, p[i], cos, sin, g_mix_pre[i], w_in[i], w_alpha_up[i], b_alpha[i],
                         g_gla_out[i], g_q_norm[i], g_k_norm[i], w_o_gla[i], w_o_attn[i],
                         w_out[i], g_mix_post[i], g_ffn_pre[i], w_ffn_in[i], w_ffn_out[i],
                         g_ffn_post[i], w_ple_proj[i], w_ple_gate[i], g_ple_post[i])
    return h
```

```python
import functools

import jax
import jax.numpy as jnp
import numpy as np
from jax import lax
from jax.experimental import pallas as pl
from jax.experimental.pallas import tpu as pltpu

F32 = jnp.float32
BF16 = jnp.bfloat16

D_MODEL = 1024
GRID_W = 64
PLE_DIM = 256
EPS = 1e-6
GLA_HEADS = 4
GLA_DK = 128
GLA_DV = 256
GLA_KEY = GLA_HEADS * GLA_DK
GLA_VAL = GLA_HEADS * GLA_DV
GLA_RANK = 16
GLA_TAU = 16.0
GLA_CHUNK = 64
ATTN_HEADS = 8
ATTN_KV_HEADS = 2
ATTN_GROUP = ATTN_HEADS // ATTN_KV_HEADS
HEAD_DIM = 128
ATTN_Q = ATTN_HEADS * HEAD_DIM
ATTN_KV = ATTN_KV_HEADS * HEAD_DIM
ROPE_THETA = 10000.0
ROPE_AXIS_DIM = HEAD_DIM // 2
FFN_HIDDEN = 2816

OFF_GV = 0
OFF_GG = 1024
OFF_AQ = 2048
OFF_GATE_A = 3072
OFF_GATE_B = 4096
OFF_GQ = 5120
OFF_GK = 5632
OFF_AK = 6144
OFF_AV = 6400
PROJ_WIDTH = 6656
RANK_PAD = 128

VMEM_LIMIT_BYTES = 56 * 1024 * 1024

NT_DIMS = (((1,), (1,)), ((), ()))
TN_DIMS = (((0,), (0,)), ((), ()))


def _rms(x, g):
    return x * lax.rsqrt(jnp.mean(x * x, axis=-1, keepdims=True) + EPS) * g


def _dot(a, b):
    return jnp.dot(a, b, preferred_element_type=F32)


def _split_bf16(x):
    hi = x.astype(BF16)
    lo = (x - hi.astype(F32)).astype(BF16)
    return hi, lo


def _resident(shape):
    zeros = (0,) * len(shape)
    return pl.BlockSpec(shape, lambda *_: zeros, pipeline_mode=pl.Buffered(1))


def _proj_kernel(x_ref, g_ref, w_ref, wr_ref, o_ref, r_ref, u_ref):
    @pl.when(pl.program_id(1) == 0)
    def _():
        u = _rms(x_ref[...], g_ref[...]).astype(BF16)
        u_ref[...] = u
        r_ref[...] = _dot(u, wr_ref[...])

    o_ref[...] = _dot(u_ref[...], w_ref[...]).astype(o_ref.dtype)


def _proj(h, g, w, w_rank, *, tm=1024, tn=512):
    t = h.shape[0]
    n = w.shape[1]
    return pl.pallas_call(
        _proj_kernel,
        grid=(t // tm, n // tn),
        in_specs=[
            pl.BlockSpec((tm, D_MODEL), lambda i, j: (i, 0)),
            pl.BlockSpec((1, D_MODEL), lambda i, j: (0, 0)),
            pl.BlockSpec((D_MODEL, tn), lambda i, j: (0, j)),
            pl.BlockSpec((D_MODEL, RANK_PAD), lambda i, j: (0, 0)),
        ],
        out_specs=[
            pl.BlockSpec((tm, tn), lambda i, j: (i, j)),
            pl.BlockSpec((tm, RANK_PAD), lambda i, j: (i, 0)),
        ],
        out_shape=[
            jax.ShapeDtypeStruct((t, n), BF16),
            jax.ShapeDtypeStruct((t, RANK_PAD), F32),
        ],
        scratch_shapes=[pltpu.VMEM((tm, D_MODEL), BF16)],
        compiler_params=pltpu.CompilerParams(
            dimension_semantics=("parallel", "arbitrary"),
            vmem_limit_bytes=VMEM_LIMIT_BYTES,
        ),
        name="proj",
    )(h, g, w, w_rank)


def _gla_kernel(q_ref, k_ref, v_ref, gg_ref, r_ref, wup_ref, b_ref, g_ref,
                o_ref, of_ref, st_ref, la_ref, ob_ref, *, rows, nblk):
    ph = pl.program_id(2)
    i = pl.program_id(3)
    fwd = ph == 0
    rb = jnp.where(fwd, i, nblk - 1 - i)
    nchunk = rows // GLA_CHUNK
    c = GLA_CHUNK

    @pl.when(i == 0)
    def _():
        st_ref[...] = jnp.zeros_like(st_ref)

    r_hi, r_lo = _split_bf16(r_ref[...])
    w_hi, w_lo = _split_bf16(wup_ref[...])
    x = _dot(r_hi, w_hi) + _dot(r_hi, w_lo) + _dot(r_lo, w_hi) + b_ref[...]
    la_ref[...] = (jnp.minimum(x, 0.0) - jnp.log(1.0 + jnp.exp(-jnp.abs(x)))) * (1.0 / GLA_TAU)

    row = lax.broadcasted_iota(jnp.int32, (c, c), 0)
    col = lax.broadcasted_iota(jnp.int32, (c, c), 1)
    tri = (col - row) * jnp.where(fwd, 1, -1) <= 0
    tri_b = jnp.where(tri, 1.0, 0.0).astype(BF16)
    scale = GLA_DK ** -0.5

    st = st_ref[...]
    for ci in range(nchunk):
        cc = jnp.where(fwd, ci, nchunk - 1 - ci)
        sl = pl.ds(pl.multiple_of(cc * c, c), c)
        la = la_ref[sl, :]
        la_hi, la_lo = _split_bf16(la)
        cum = _dot(tri_b, la_hi) + _dot(tri_b, la_lo)
        e_tot = jnp.exp(jnp.sum(la, axis=0, keepdims=True))
        qd = (q_ref[sl, :].astype(F32) * (scale * jnp.exp(cum))).astype(BF16)
        k_in = k_ref[sl, :].astype(F32) * jnp.exp(-cum)
        vc = v_ref[sl, :]
        sc = lax.dot_general(qd, k_in.astype(BF16), NT_DIMS, preferred_element_type=F32)
        sc = jnp.where(tri, sc, 0.0).astype(BF16)
        o = _dot(sc, vc) + lax.dot_general(qd, st.astype(BF16), NT_DIMS, preferred_element_type=F32)
        ob_ref[sl, :] = o
        k_st = (k_in * e_tot).astype(BF16)
        st = st * e_tot + lax.dot_general(vc, k_st, TN_DIMS, preferred_element_type=F32)
    st_ref[...] = st

    blk = pl.ds(pl.multiple_of(rb * rows, rows), rows)

    @pl.when(fwd)
    def _():
        of_ref[blk, :] = ob_ref[...]

    @pl.when(jnp.logical_not(fwd))
    def _():
        y = _rms(of_ref[blk, :] + ob_ref[...], g_ref[...])
        gg = gg_ref[...].astype(F32)
        o_ref[...] = (y * (gg * jax.nn.sigmoid(gg))).astype(o_ref.dtype)


def _gla(proj, rank, wup_pad, b_alpha, g_out, *, batch, seq, rows=512):
    nblk = seq // rows
    t = batch * seq

    def row_block(b, ph, i):
        return b * nblk + jnp.where(ph == 0, i, nblk - 1 - i)

    def parked_block(b, ph, i):
        return b * nblk + jnp.where(ph == 0, nblk - 1, nblk - 1 - i)

    kern = functools.partial(_gla_kernel, rows=rows, nblk=nblk)
    return pl.pallas_call(
        kern,
        grid=(batch, GLA_HEADS, 2, nblk),
        in_specs=[
            pl.BlockSpec((rows, GLA_DK), lambda b, h, ph, i: (row_block(b, ph, i), OFF_GQ // GLA_DK + h)),
            pl.BlockSpec((rows, GLA_DK), lambda b, h, ph, i: (row_block(b, ph, i), OFF_GK // GLA_DK + h)),
            pl.BlockSpec((rows, GLA_DV), lambda b, h, ph, i: (row_block(b, ph, i), OFF_GV // GLA_DV + h)),
            pl.BlockSpec((rows, GLA_DV), lambda b, h, ph, i: (parked_block(b, ph, i), OFF_GG // GLA_DV + h)),
            pl.BlockSpec((rows, RANK_PAD), lambda b, h, ph, i: (row_block(b, ph, i), 0)),
            pl.BlockSpec((None, RANK_PAD, GLA_DK), lambda b, h, ph, i: (ph, 0, h)),
            pl.BlockSpec((None, 1, GLA_DK), lambda b, h, ph, i: (ph, 0, h)),
            pl.BlockSpec((1, GLA_DV), lambda b, h, ph, i: (0, 0)),
        ],
        out_specs=pl.BlockSpec((rows, GLA_DV), lambda b, h, ph, i: (parked_block(b, ph, i), h)),
        out_shape=jax.ShapeDtypeStruct((t, GLA_VAL), BF16),
        scratch_shapes=[
            pltpu.VMEM((seq, GLA_DV), F32),
            pltpu.VMEM((GLA_DV, GLA_DK), F32),
            pltpu.VMEM((rows, GLA_DK), F32),
            pltpu.VMEM((rows, GLA_DV), F32),
        ],
        compiler_params=pltpu.CompilerParams(
            dimension_semantics=("parallel", "parallel", "arbitrary", "arbitrary"),
            vmem_limit_bytes=VMEM_LIMIT_BYTES,
        ),
        name="gla",
    )(proj, proj, proj, proj, rank, wup_pad, b_alpha, g_out)


def _attn_kernel(q_ref, k_ref, v_ref, cos_ref, sa_ref, sb_ref, gq_ref, gk_ref,
                 o_ref, kn_ref, *, tq):
    qi = pl.program_id(2)

    def rope(x, rows):
        return (x * cos_ref[rows, :]
                + pltpu.roll(x, HEAD_DIM - 32, 1) * sa_ref[rows, :]
                + pltpu.roll(x, 32, 1) * sb_ref[rows, :])

    @pl.when(qi == 0)
    def _():
        k = _rms(k_ref[...].astype(F32), gk_ref[...])
        kn_ref[...] = rope(k, slice(None)).astype(BF16)

    rows = pl.ds(pl.multiple_of(qi * tq, tq), tq)
    scale = HEAD_DIM ** -0.5
    qs = []
    for hh in range(ATTN_GROUP):
        x = _rms(q_ref[:, hh * HEAD_DIM:(hh + 1) * HEAD_DIM].astype(F32), gq_ref[...])
        qs.append((rope(x, rows) * scale).astype(BF16))
    q = jnp.concatenate(qs, axis=0)
    s = lax.dot_general(q, kn_ref[...], NT_DIMS, preferred_element_type=F32)
    m = jnp.max(s, axis=-1, keepdims=True)
    p = jnp.exp(s - m)
    l = jnp.sum(p, axis=-1, keepdims=True)
    o = _dot(p.astype(BF16), v_ref[...]) / l
    for hh in range(ATTN_GROUP):
        o_ref[:, hh * HEAD_DIM:(hh + 1) * HEAD_DIM] = o[hh * tq:(hh + 1) * tq].astype(o_ref.dtype)


def _attn(proj, cos_t, sa_t, sb_t, g_q, g_k, *, batch, seq, tq=128):
    nq = seq // tq
    t = batch * seq
    gw = ATTN_GROUP * HEAD_DIM
    kern = functools.partial(_attn_kernel, tq=tq)
    return pl.pallas_call(
        kern,
        grid=(batch, ATTN_KV_HEADS, nq),
        in_specs=[
            pl.BlockSpec((tq, gw), lambda b, g, i: (b * nq + i, OFF_AQ // gw + g)),
            pl.BlockSpec((seq, HEAD_DIM), lambda b, g, i: (b, OFF_AK // HEAD_DIM + g)),
            pl.BlockSpec((seq, HEAD_DIM), lambda b, g, i: (b, OFF_AV // HEAD_DIM + g)),
            _resident((seq, HEAD_DIM)),
            _resident((seq, HEAD_DIM)),
            _resident((seq, HEAD_DIM)),
            pl.BlockSpec((1, HEAD_DIM), lambda b, g, i: (0, 0)),
            pl.BlockSpec((1, HEAD_DIM), lambda b, g, i: (0, 0)),
        ],
        out_specs=pl.BlockSpec((tq, gw), lambda b, g, i: (b * nq + i, g)),
        out_shape=jax.ShapeDtypeStruct((t, ATTN_Q), BF16),
        scratch_shapes=[pltpu.VMEM((seq, HEAD_DIM), BF16)],
        compiler_params=pltpu.CompilerParams(
            dimension_semantics=("parallel", "parallel", "arbitrary"),
            vmem_limit_bytes=VMEM_LIMIT_BYTES,
        ),
        name="attn",
    )(proj, proj, proj, cos_t, sa_t, sb_t, g_q, g_k)


def _merge_kernel(a_ref, at_ref, ga_ref, gb_ref, h_ref, woa_ref, wob_ref, wout_ref, g_ref, o_ref):
    branch_a = _dot(a_ref[...], woa_ref[...])
    branch_b = _dot(at_ref[...], wob_ref[...])
    mixed = (jax.nn.sigmoid(ga_ref[...].astype(F32)) * branch_a
             + jax.nn.sigmoid(gb_ref[...].astype(F32)) * branch_b)
    y = _dot(mixed.astype(BF16), wout_ref[...])
    o_ref[...] = h_ref[...] + _rms(y, g_ref[...])


def _merge(gla_out, attn_out, proj, h, w_o_gla, w_o_attn, w_out, g_post, *, tm=512):
    t = h.shape[0]
    row = lambda i: (i, 0)
    return pl.pallas_call(
        _merge_kernel,
        grid=(t // tm,),
        in_specs=[
            pl.BlockSpec((tm, GLA_VAL), row),
            pl.BlockSpec((tm, ATTN_Q), row),
            pl.BlockSpec((tm, D_MODEL), lambda i: (i, OFF_GATE_A // D_MODEL)),
            pl.BlockSpec((tm, D_MODEL), lambda i: (i, OFF_GATE_B // D_MODEL)),
            pl.BlockSpec((tm, D_MODEL), row),
            _resident((GLA_VAL, D_MODEL)),
            _resident((ATTN_Q, D_MODEL)),
            _resident((D_MODEL, D_MODEL)),
            _resident((1, D_MODEL)),
        ],
        out_specs=pl.BlockSpec((tm, D_MODEL), row),
        out_shape=jax.ShapeDtypeStruct((t, D_MODEL), F32),
        compiler_params=pltpu.CompilerParams(
            dimension_semantics=("parallel",),
            vmem_limit_bytes=VMEM_LIMIT_BYTES,
        ),
        name="merge",
    )(gla_out, attn_out, proj, proj, h, w_o_gla, w_o_attn, w_out, g_post)


def _ffn_kernel(h_ref, p_ref, gpre_ref, win_ref, wout_ref, gpost_ref, wpp_ref, wpg_ref, gple_ref,
                o_ref, *, hidden_chunk):
    h = h_ref[...]
    x = _rms(h, gpre_ref[...]).astype(BF16)
    acc = jnp.zeros(h.shape, F32)
    for c0 in range(0, FFN_HIDDEN, hidden_chunk):
        gate = _dot(x, win_ref[:, c0:c0 + hidden_chunk])
        up = _dot(x, win_ref[:, FFN_HIDDEN + c0:FFN_HIDDEN + c0 + hidden_chunk])
        act = (gate * jax.nn.sigmoid(gate)) * up
        acc = acc + _dot(act.astype(BF16), wout_ref[c0:c0 + hidden_chunk, :])
    h = h + _rms(acc, gpost_ref[...])
    e = _dot(p_ref[...].astype(BF16), wpp_ref[...])
    gate = jax.nn.sigmoid(_dot(h.astype(BF16), wpg_ref[...]))
    o_ref[...] = h + _rms(gate * e, gple_ref[...])


def _ffn(h, p_i, g_pre, w_in, w_out, g_post, w_pp, w_pg, g_ple, *, tm=512, hidden_chunk=1408):
    t = h.shape[0]
    row = lambda i: (i, 0)
    kern = functools.partial(_ffn_kernel, hidden_chunk=hidden_chunk)
    return pl.pallas_call(
        kern,
        grid=(t // tm,),
        in_specs=[
            pl.BlockSpec((tm, D_MODEL), row),
            pl.BlockSpec((tm, PLE_DIM), row),
            _resident((1, D_MODEL)),
            _resident((D_MODEL, 2 * FFN_HIDDEN)),
            _resident((FFN_HIDDEN, D_MODEL)),
            _resident((1, D_MODEL)),
            _resident((PLE_DIM, D_MODEL)),
            _resident((D_MODEL, D_MODEL)),
            _resident((1, D_MODEL)),
        ],
        out_specs=pl.BlockSpec((tm, D_MODEL), row),
        out_shape=jax.ShapeDtypeStruct((t, D_MODEL), F32),
        compiler_params=pltpu.CompilerParams(
            dimension_semantics=("parallel",),
            vmem_limit_bytes=VMEM_LIMIT_BYTES,
        ),
        name="ffn",
    )(h, p_i, g_pre, w_in, w_out, g_post, w_pp, w_pg, g_ple)


def _rope_tables(seq):
    rows = seq // GRID_W
    row = jnp.repeat(jnp.arange(rows, dtype=F32), GRID_W)
    col = jnp.tile(jnp.arange(GRID_W, dtype=F32), rows)
    inv = ROPE_THETA ** (-jnp.arange(0, ROPE_AXIS_DIM, 2, dtype=F32) / ROPE_AXIS_DIM)
    ang = jnp.stack([row[:, None] * inv, col[:, None] * inv], axis=1)
    cos, sin = jnp.cos(ang), jnp.sin(ang)
    zero = jnp.zeros_like(sin)
    lanes = lambda a, b: jnp.stack([a, b], axis=2).reshape(seq, HEAD_DIM)
    return lanes(cos, cos), lanes(-sin, zero), lanes(zero, sin)


def _permute_w_in(w_in):
    sizes = (GLA_KEY, GLA_KEY, GLA_VAL, GLA_VAL, GLA_RANK, GLA_RANK,
             ATTN_Q, ATTN_KV, ATTN_KV, D_MODEL, D_MODEL)
    offs = np.cumsum((0,) + sizes)
    gq, gk, gv, gg, ra_f, ra_b, aq, ak, av, gate_a, gate_b = (
        w_in[:, offs[n]:offs[n + 1]] for n in range(len(sizes)))
    main = jnp.concatenate([gv, gg, aq, gate_a, gate_b, gq, gk, ak, av], axis=1)
    rank = jnp.concatenate([ra_f, ra_b], axis=1)
    rank = jnp.pad(rank, ((0, 0), (0, RANK_PAD - 2 * GLA_RANK)))
    return main.astype(BF16), rank.astype(BF16)


def _pad_w_alpha(w_alpha_up):
    out = jnp.zeros((2, RANK_PAD, GLA_KEY), F32)
    for d in range(2):
        out = out.at[d, d * GLA_RANK:(d + 1) * GLA_RANK, :].set(w_alpha_up[d])
    return out


def kernel(x, p, g_mix_pre, w_in, w_alpha_up, b_alpha, g_gla_out, g_q_norm, g_k_norm, w_o_gla,
           w_o_attn, w_out, g_mix_post, g_ffn_pre, w_ffn_in, w_ffn_out, g_ffn_post, w_ple_proj,
           w_ple_gate, g_ple_post):
    batch, seq, _ = x.shape
    depth = p.shape[0]
    t = batch * seq
    cos_t, sa_t, sb_t = _rope_tables(seq)
    h = x.reshape(t, D_MODEL)
    row = lambda v: v.reshape(1, -1)
    for i in range(depth):
        w_main, w_rank = _permute_w_in(w_in[i])
        proj, rank = _proj(h, row(g_mix_pre[i]), w_main, w_rank)
        gla_out = _gla(proj, rank, _pad_w_alpha(w_alpha_up[i]), b_alpha[i].reshape(2, 1, GLA_KEY),
                       row(g_gla_out[i]), batch=batch, seq=seq)
        attn_out = _attn(proj, cos_t, sa_t, sb_t, row(g_q_norm[i]), row(g_k_norm[i]),
                         batch=batch, seq=seq)
        h = _merge(gla_out, attn_out, proj, h, w_o_gla[i].astype(BF16), w_o_attn[i].astype(BF16),
                   w_out[i].astype(BF16), row(g_mix_post[i]))
        h = _ffn(h, p[i].reshape(t, PLE_DIM), row(g_ffn_pre[i]), w_ffn_in[i].astype(BF16),
                 w_ffn_out[i].astype(BF16), row(g_ffn_post[i]), w_ple_proj[i].astype(BF16),
                 w_ple_gate[i].astype(BF16), row(g_ple_post[i]))
    return h.reshape(batch, seq, D_MODEL)
```

```python
import functools

import jax
import jax.numpy as jnp
import numpy as np
from jax import lax
from jax.experimental import pallas as pl
from jax.experimental.pallas import tpu as pltpu

F32 = jnp.float32
BF16 = jnp.bfloat16

D_MODEL = 1024
GRID_W = 64
PLE_DIM = 256
EPS = 1e-6
GLA_HEADS = 4
GLA_DK = 128
GLA_DV = 256
GLA_KEY = GLA_HEADS * GLA_DK
GLA_VAL = GLA_HEADS * GLA_DV
GLA_RANK = 16
GLA_TAU = 16.0
GLA_CHUNK = 64
GLA_CHUNK_SHIFT = 6
assert 1 << GLA_CHUNK_SHIFT == GLA_CHUNK
ATTN_HEADS = 8
ATTN_KV_HEADS = 2
ATTN_GROUP = ATTN_HEADS // ATTN_KV_HEADS
HEAD_DIM = 128
ATTN_Q = ATTN_HEADS * HEAD_DIM
ATTN_KV = ATTN_KV_HEADS * HEAD_DIM
ROPE_THETA = 10000.0
ROPE_AXIS_DIM = HEAD_DIM // 2
FFN_HIDDEN = 2816
LOG2_E = 1.4426950408889634

OFF_GV = 0
OFF_GG = 1024
OFF_AQ = 2048
OFF_GATE_A = 3072
OFF_GATE_B = 4096
OFF_GQ = 5120
OFF_GK = 5632
OFF_AK = 6144
OFF_AV = 6400
PROJ_WIDTH = 6656
RANK_PAD = 128

VMEM_LIMIT_BYTES = 56 * 1024 * 1024

NT_DIMS = (((1,), (1,)), ((), ()))
TN_DIMS = (((0,), (0,)), ((), ()))


def _rms(x, g):
    return x * lax.rsqrt(jnp.mean(x * x, axis=-1, keepdims=True) + EPS) * g


def _dot(a, b):
    return jnp.dot(a, b, preferred_element_type=F32)


def _split_bf16(x):
    hi = x.astype(BF16)
    lo = (x - hi.astype(F32)).astype(BF16)
    return hi, lo


def _resident(shape):
    zeros = (0,) * len(shape)
    return pl.BlockSpec(shape, lambda *_: zeros, pipeline_mode=pl.Buffered(1))


def _proj_kernel(x_ref, g_ref, w_ref, wr_ref, o_ref, r_ref, u_ref):
    @pl.when(pl.program_id(1) == 0)
    def _():
        u = _rms(x_ref[...], g_ref[...]).astype(BF16)
        u_ref[...] = u
        r_ref[...] = _dot(u, wr_ref[...])

    o_ref[...] = _dot(u_ref[...], w_ref[...]).astype(o_ref.dtype)


def _proj(h, g, w, w_rank, *, tm=1024, tn=512):
    t = h.shape[0]
    n = w.shape[1]
    return pl.pallas_call(
        _proj_kernel,
        grid=(t // tm, n // tn),
        in_specs=[
            pl.BlockSpec((tm, D_MODEL), lambda i, j: (i, 0)),
            pl.BlockSpec((1, D_MODEL), lambda i, j: (0, 0)),
            pl.BlockSpec((D_MODEL, tn), lambda i, j: (0, j)),
            pl.BlockSpec((D_MODEL, RANK_PAD), lambda i, j: (0, 0)),
        ],
        out_specs=[
            pl.BlockSpec((tm, tn), lambda i, j: (i, j)),
            pl.BlockSpec((tm, RANK_PAD), lambda i, j: (i, 0)),
        ],
        out_shape=[
            jax.ShapeDtypeStruct((t, n), BF16),
            jax.ShapeDtypeStruct((t, RANK_PAD), F32),
        ],
        scratch_shapes=[pltpu.VMEM((tm, D_MODEL), BF16)],
        compiler_params=pltpu.CompilerParams(
            dimension_semantics=("parallel", "arbitrary"),
            vmem_limit_bytes=VMEM_LIMIT_BYTES,
        ),
        name="proj",
    )(h, g, w, w_rank)


def _gla_block(direction, q_ref, k_ref, v_ref, gg_ref, r_ref, wup_ref, b_ref, g_ref,
               o_ref, of_ref, st_ref, *, rows, sub, seq_row0):
    c = GLA_CHUNK
    nsub = rows // sub
    nch = sub // c
    forward = direction > 0
    row = lax.broadcasted_iota(jnp.int32, (sub, sub), 0)
    col = lax.broadcasted_iota(jnp.int32, (sub, sub), 1)
    same = (row >> GLA_CHUNK_SHIFT) == (col >> GLA_CHUNK_SHIFT)
    tri = jnp.logical_and(same, (col <= row) if forward else (col >= row))
    sum_mat = jnp.concatenate([jnp.where(tri, 1.0, 0.0), jnp.where(same, 1.0, 0.0)], axis=0).astype(BF16)
    w_hi, w_lo = _split_bf16(wup_ref[...])
    scale = GLA_DK ** -0.5

    for si in (range(nsub) if forward else range(nsub - 1, -1, -1)):
        rs = slice(si * sub, (si + 1) * sub)
        r_hi, r_lo = _split_bf16(r_ref[rs, :])
        x = _dot(r_hi, w_hi) + _dot(r_hi, w_lo) + _dot(r_lo, w_hi) + b_ref[...]
        la = (jnp.minimum(x, 0.0) - jnp.log(1.0 + jnp.exp(-jnp.abs(x)))) * (1.0 / GLA_TAU)
        la_hi, la_lo = _split_bf16(la)
        sums = _dot(sum_mat, la_hi) + _dot(sum_mat, la_lo)
        cum, tot = sums[:sub], sums[sub:]
        e_tot = jnp.exp(tot)
        qd = (q_ref[rs, :].astype(F32) * (scale * jnp.exp(cum))).astype(BF16)
        k_in = k_ref[rs, :].astype(F32) * jnp.exp(-cum)
        k_st = (k_in * e_tot).astype(BF16)
        k_in = k_in.astype(BF16)
        for h in range(GLA_HEADS):
            ks = slice(h * GLA_DK, (h + 1) * GLA_DK)
            vs = slice(h * GLA_DV, (h + 1) * GLA_DV)
            v = v_ref[rs, vs]
            sc = lax.dot_general(qd[:, ks], k_in[:, ks], NT_DIMS, preferred_element_type=F32)
            o = _dot(jnp.where(tri, sc, 0.0).astype(BF16), v)
            st = st_ref[h]
            inter = [None] * nch
            for ci in (range(nch) if forward else range(nch - 1, -1, -1)):
                cr = slice(ci * c, (ci + 1) * c)
                inter[ci] = lax.dot_general(qd[cr, ks], st.astype(BF16), NT_DIMS, preferred_element_type=F32)
                upd = lax.dot_general(v[cr, :], k_st[cr, ks], TN_DIMS, preferred_element_type=F32)
                st = st * e_tot[ci * c:ci * c + 1, ks] + upd
            st_ref[h] = st
            o = o + jnp.concatenate(inter, axis=0)
            seq_rows = pl.ds(pl.multiple_of(seq_row0 + si * sub, sub), sub)
            if forward:
                of_ref[seq_rows, vs] = o
            else:
                y = _rms(of_ref[seq_rows, vs] + o, g_ref[...])
                gg = gg_ref[rs, vs].astype(F32)
                o_ref[rs, vs] = (y * (gg * jax.nn.sigmoid(gg))).astype(o_ref.dtype)


def _gla_kernel(q_ref, k_ref, v_ref, gg_ref, r_ref, wup_ref, b_ref, g_ref,
                o_ref, of_ref, st_ref, *, rows, nblk, sub):
    ph = pl.program_id(1)
    i = pl.program_id(2)

    @pl.when(i == 0)
    def _():
        st_ref[...] = jnp.zeros_like(st_ref)

    refs = (q_ref, k_ref, v_ref, gg_ref, r_ref, wup_ref, b_ref, g_ref, o_ref, of_ref, st_ref)

    @pl.when(ph == 0)
    def _():
        _gla_block(1, *refs, rows=rows, sub=sub, seq_row0=i * rows)

    @pl.when(ph == 1)
    def _():
        _gla_block(-1, *refs, rows=rows, sub=sub, seq_row0=(nblk - 1 - i) * rows)


def _gla(proj, rank, wup_pad, b_alpha, g_out, *, batch, seq, rows=512, sub=256):
    nblk = seq // rows
    t = batch * seq

    def row_block(b, ph, i):
        return b * nblk + jnp.where(ph == 0, i, nblk - 1 - i)

    def parked_block(b, ph, i):
        return b * nblk + jnp.where(ph == 0, nblk - 1, nblk - 1 - i)

    kern = functools.partial(_gla_kernel, rows=rows, nblk=nblk, sub=sub)
    return pl.pallas_call(
        kern,
        grid=(batch, 2, nblk),
        in_specs=[
            pl.BlockSpec((rows, GLA_KEY), lambda b, ph, i: (row_block(b, ph, i), OFF_GQ // GLA_KEY)),
            pl.BlockSpec((rows, GLA_KEY), lambda b, ph, i: (row_block(b, ph, i), OFF_GK // GLA_KEY)),
            pl.BlockSpec((rows, GLA_VAL), lambda b, ph, i: (row_block(b, ph, i), OFF_GV // GLA_VAL)),
            pl.BlockSpec((rows, GLA_VAL), lambda b, ph, i: (parked_block(b, ph, i), OFF_GG // GLA_VAL)),
            pl.BlockSpec((rows, RANK_PAD), lambda b, ph, i: (row_block(b, ph, i), 0)),
            pl.BlockSpec((None, RANK_PAD, GLA_KEY), lambda b, ph, i: (ph, 0, 0)),
            pl.BlockSpec((None, 1, GLA_KEY), lambda b, ph, i: (ph, 0, 0)),
            pl.BlockSpec((1, GLA_DV), lambda b, ph, i: (0, 0)),
        ],
        out_specs=pl.BlockSpec((rows, GLA_VAL), lambda b, ph, i: (parked_block(b, ph, i), 0)),
        out_shape=jax.ShapeDtypeStruct((t, GLA_VAL), BF16),
        scratch_shapes=[
            pltpu.VMEM((seq, GLA_VAL), F32),
            pltpu.VMEM((GLA_HEADS, GLA_DV, GLA_DK), F32),
        ],
        compiler_params=pltpu.CompilerParams(
            dimension_semantics=("parallel", "arbitrary", "arbitrary"),
            vmem_limit_bytes=VMEM_LIMIT_BYTES,
        ),
        name="gla",
    )(proj, proj, proj, proj, rank, wup_pad, b_alpha, g_out)


def _attn_kernel(q_ref, k_ref, v_ref, cos_ref, sa_ref, sb_ref, gq_ref, gk_ref,
                 o_ref, kn_ref, v1_ref, s_ref, *, tq, kc):
    qi = pl.program_id(2)
    seq = kn_ref.shape[0]

    def rope(x, rows):
        return (x * cos_ref[rows, :]
                + pltpu.roll(x, HEAD_DIM - 32, 1) * sa_ref[rows, :]
                + pltpu.roll(x, 32, 1) * sb_ref[rows, :])

    @pl.when(qi == 0)
    def _():
        k = _rms(k_ref[...].astype(F32), gk_ref[...])
        kn_ref[...] = rope(k, slice(None)).astype(BF16)
        v1_ref[:, :HEAD_DIM] = v_ref[...]
        v1_ref[:, HEAD_DIM:] = jnp.ones((seq, HEAD_DIM), BF16)

    rows = pl.ds(pl.multiple_of(qi * tq, tq), tq)
    scale = HEAD_DIM ** -0.5 * LOG2_E
    for hh in range(ATTN_GROUP):
        lanes = slice(hh * HEAD_DIM, (hh + 1) * HEAD_DIM)
        x = _rms(q_ref[:, lanes].astype(F32), gq_ref[...])
        q = (rope(x, rows) * scale).astype(BF16)
        m = jnp.full((tq, HEAD_DIM), -jnp.inf, F32)
        for c0 in range(0, seq, kc):
            s = lax.dot_general(q, kn_ref[c0:c0 + kc, :], NT_DIMS, preferred_element_type=F32)
            s_ref[hh, :, c0:c0 + kc] = s
            for l0 in range(0, kc, HEAD_DIM):
                m = jnp.maximum(m, s[:, l0:l0 + HEAD_DIM])
        m = jnp.max(m, axis=-1, keepdims=True)
        acc = jnp.zeros((tq, 2 * HEAD_DIM), F32)
        for c0 in range(0, seq, kc):
            p = jnp.exp2((s_ref[hh, :, c0:c0 + kc] - m).astype(BF16))
            acc = acc + _dot(p, v1_ref[c0:c0 + kc, :])
        o_ref[:, lanes] = (acc[:, :HEAD_DIM] / acc[:, HEAD_DIM:HEAD_DIM + 1]).astype(o_ref.dtype)


def _attn(proj, cos_t, sa_t, sb_t, g_q, g_k, *, batch, seq, tq=128, kc=512):
    nq = seq // tq
    t = batch * seq
    gw = ATTN_GROUP * HEAD_DIM
    kern = functools.partial(_attn_kernel, tq=tq, kc=kc)
    return pl.pallas_call(
        kern,
        grid=(batch, ATTN_KV_HEADS, nq),
        in_specs=[
            pl.BlockSpec((tq, gw), lambda b, g, i: (b * nq + i, OFF_AQ // gw + g)),
            pl.BlockSpec((seq, HEAD_DIM), lambda b, g, i: (b, OFF_AK // HEAD_DIM + g)),
            pl.BlockSpec((seq, HEAD_DIM), lambda b, g, i: (b, OFF_AV // HEAD_DIM + g)),
            _resident((seq, HEAD_DIM)),
            _resident((seq, HEAD_DIM)),
            _resident((seq, HEAD_DIM)),
            pl.BlockSpec((1, HEAD_DIM), lambda b, g, i: (0, 0)),
            pl.BlockSpec((1, HEAD_DIM), lambda b, g, i: (0, 0)),
        ],
        out_specs=pl.BlockSpec((tq, gw), lambda b, g, i: (b * nq + i, g)),
        out_shape=jax.ShapeDtypeStruct((t, ATTN_Q), BF16),
        scratch_shapes=[
            pltpu.VMEM((seq, HEAD_DIM), BF16),
            pltpu.VMEM((seq, 2 * HEAD_DIM), BF16),
            pltpu.VMEM((ATTN_GROUP, tq, seq), F32),
        ],
        compiler_params=pltpu.CompilerParams(
            dimension_semantics=("parallel", "parallel", "arbitrary"),
            vmem_limit_bytes=VMEM_LIMIT_BYTES,
        ),
        name="attn",
    )(proj, proj, proj, cos_t, sa_t, sb_t, g_q, g_k)


def _merge_kernel(a_ref, at_ref, ga_ref, gb_ref, h_ref, woa_ref, wob_ref, wout_ref, g_ref, o_ref):
    branch_a = _dot(a_ref[...], woa_ref[...])
    branch_b = _dot(at_ref[...], wob_ref[...])
    mixed = (jax.nn.sigmoid(ga_ref[...].astype(F32)) * branch_a
             + jax.nn.sigmoid(gb_ref[...].astype(F32)) * branch_b)
    y = _dot(mixed.astype(BF16), wout_ref[...])
    o_ref[...] = h_ref[...] + _rms(y, g_ref[...])


def _merge(gla_out, attn_out, proj, h, w_o_gla, w_o_attn, w_out, g_post, *, tm=512):
    t = h.shape[0]
    row = lambda i: (i, 0)
    return pl.pallas_call(
        _merge_kernel,
        grid=(t // tm,),
        in_specs=[
            pl.BlockSpec((tm, GLA_VAL), row),
            pl.BlockSpec((tm, ATTN_Q), row),
            pl.BlockSpec((tm, D_MODEL), lambda i: (i, OFF_GATE_A // D_MODEL)),
            pl.BlockSpec((tm, D_MODEL), lambda i: (i, OFF_GATE_B // D_MODEL)),
            pl.BlockSpec((tm, D_MODEL), row),
            _resident((GLA_VAL, D_MODEL)),
            _resident((ATTN_Q, D_MODEL)),
            _resident((D_MODEL, D_MODEL)),
            _resident((1, D_MODEL)),
        ],
        out_specs=pl.BlockSpec((tm, D_MODEL), row),
        out_shape=jax.ShapeDtypeStruct((t, D_MODEL), F32),
        compiler_params=pltpu.CompilerParams(
            dimension_semantics=("parallel",),
            vmem_limit_bytes=VMEM_LIMIT_BYTES,
        ),
        name="merge",
    )(gla_out, attn_out, proj, proj, h, w_o_gla, w_o_attn, w_out, g_post)


def _ffn_kernel(h_ref, p_ref, gpre_ref, win_ref, wout_ref, gpost_ref, wpp_ref, wpg_ref, gple_ref,
                o_ref, *, hidden_chunk):
    h = h_ref[...]
    x = _rms(h, gpre_ref[...]).astype(BF16)
    acc = jnp.zeros(h.shape, F32)
    for c0 in range(0, FFN_HIDDEN, hidden_chunk):
        gate = _dot(x, win_ref[:, c0:c0 + hidden_chunk])
        up = _dot(x, win_ref[:, FFN_HIDDEN + c0:FFN_HIDDEN + c0 + hidden_chunk])
        act = (gate * jax.nn.sigmoid(gate)) * up
        acc = acc + _dot(act.astype(BF16), wout_ref[c0:c0 + hidden_chunk, :])
    h = h + _rms(acc, gpost_ref[...])
    e = _dot(p_ref[...].astype(BF16), wpp_ref[...])
    gate = jax.nn.sigmoid(_dot(h.astype(BF16), wpg_ref[...]))
    o_ref[...] = h + _rms(gate * e, gple_ref[...])


def _ffn(h, p_i, g_pre, w_in, w_out, g_post, w_pp, w_pg, g_ple, *, tm=512, hidden_chunk=1408):
    t = h.shape[0]
    row = lambda i: (i, 0)
    kern = functools.partial(_ffn_kernel, hidden_chunk=hidden_chunk)
    return pl.pallas_call(
        kern,
        grid=(t // tm,),
        in_specs=[
            pl.BlockSpec((tm, D_MODEL), row),
            pl.BlockSpec((tm, PLE_DIM), row),
            _resident((1, D_MODEL)),
            _resident((D_MODEL, 2 * FFN_HIDDEN)),
            _resident((FFN_HIDDEN, D_MODEL)),
            _resident((1, D_MODEL)),
            _resident((PLE_DIM, D_MODEL)),
            _resident((D_MODEL, D_MODEL)),
            _resident((1, D_MODEL)),
        ],
        out_specs=pl.BlockSpec((tm, D_MODEL), row),
        out_shape=jax.ShapeDtypeStruct((t, D_MODEL), F32),
        compiler_params=pltpu.CompilerParams(
            dimension_semantics=("parallel",),
            vmem_limit_bytes=VMEM_LIMIT_BYTES,
        ),
        name="ffn",
    )(h, p_i, g_pre, w_in, w_out, g_post, w_pp, w_pg, g_ple)


def _rope_tables(seq):
    rows = seq // GRID_W
    row = jnp.repeat(jnp.arange(rows, dtype=F32), GRID_W)
    col = jnp.tile(jnp.arange(GRID_W, dtype=F32), rows)
    inv = ROPE_THETA ** (-jnp.arange(0, ROPE_AXIS_DIM, 2, dtype=F32) / ROPE_AXIS_DIM)
    ang = jnp.stack([row[:, None] * inv, col[:, None] * inv], axis=1)
    cos, sin = jnp.cos(ang), jnp.sin(ang)
    zero = jnp.zeros_like(sin)
    lanes = lambda a, b: jnp.stack([a, b], axis=2).reshape(seq, HEAD_DIM)
    return lanes(cos, cos), lanes(-sin, zero), lanes(zero, sin)


def _permute_w_in(w_in):
    sizes = (GLA_KEY, GLA_KEY, GLA_VAL, GLA_VAL, GLA_RANK, GLA_RANK,
             ATTN_Q, ATTN_KV, ATTN_KV, D_MODEL, D_MODEL)
    offs = np.cumsum((0,) + sizes)
    gq, gk, gv, gg, ra_f, ra_b, aq, ak, av, gate_a, gate_b = (
        w_in[:, offs[n]:offs[n + 1]] for n in range(len(sizes)))
    main = jnp.concatenate([gv, gg, aq, gate_a, gate_b, gq, gk, ak, av], axis=1)
    rank = jnp.concatenate([ra_f, ra_b], axis=1)
    rank = jnp.pad(rank, ((0, 0), (0, RANK_PAD - 2 * GLA_RANK)))
    return main.astype(BF16), rank.astype(BF16)


def _pad_w_alpha(w_alpha_up):
    out = jnp.zeros((2, RANK_PAD, GLA_KEY), F32)
    for d in range(2):
        out = out.at[d, d * GLA_RANK:(d + 1) * GLA_RANK, :].set(w_alpha_up[d])
    return out


def kernel(x, p, g_mix_pre, w_in, w_alpha_up, b_alpha, g_gla_out, g_q_norm, g_k_norm, w_o_gla,
           w_o_attn, w_out, g_mix_post, g_ffn_pre, w_ffn_in, w_ffn_out, g_ffn_post, w_ple_proj,
           w_ple_gate, g_ple_post):
    batch, seq, _ = x.shape
    depth = p.shape[0]
    t = batch * seq
    cos_t, sa_t, sb_t = _rope_tables(seq)
    h = x.reshape(t, D_MODEL)
    row = lambda v: v.reshape(1, -1)
    for i in range(depth):
        w_main, w_rank = _permute_w_in(w_in[i])
        proj, rank = _proj(h, row(g_mix_pre[i]), w_main, w_rank)
        gla_out = _gla(proj, rank, _pad_w_alpha(w_alpha_up[i]), b_alpha[i].reshape(2, 1, GLA_KEY),
                       row(g_gla_out[i]), batch=batch, seq=seq)
        attn_out = _attn(proj, cos_t, sa_t, sb_t, row(g_q_norm[i]), row(g_k_norm[i]),
                         batch=batch, seq=seq)
        h = _merge(gla_out, attn_out, proj, h, w_o_gla[i].astype(BF16), w_o_attn[i].astype(BF16),
                   w_out[i].astype(BF16), row(g_mix_post[i]))
        h = _ffn(h, p[i].reshape(t, PLE_DIM), row(g_ffn_pre[i]), w_ffn_in[i].astype(BF16),
                 w_ffn_out[i].astype(BF16), row(g_ffn_post[i]), w_ple_proj[i].astype(BF16),
                 w_ple_gate[i].astype(BF16), row(g_ple_post[i]))
    return h.reshape(batch, seq, D_MODEL)
```

```python
import functools

import jax
import jax.numpy as jnp
import numpy as np
from jax import lax
from jax.experimental import pallas as pl
from jax.experimental.pallas import tpu as pltpu

F32 = jnp.float32
BF16 = jnp.bfloat16

D_MODEL = 1024
GRID_W = 64
PLE_DIM = 256
EPS = 1e-6
GLA_HEADS = 4
GLA_DK = 128
GLA_DV = 256
GLA_KEY = GLA_HEADS * GLA_DK
GLA_VAL = GLA_HEADS * GLA_DV
GLA_RANK = 16
GLA_TAU = 16.0
GLA_CHUNK = 64
GLA_CHUNK_SHIFT = 6
assert 1 << GLA_CHUNK_SHIFT == GLA_CHUNK
ATTN_HEADS = 8
ATTN_KV_HEADS = 2
ATTN_GROUP = ATTN_HEADS // ATTN_KV_HEADS
HEAD_DIM = 128
ATTN_Q = ATTN_HEADS * HEAD_DIM
ATTN_KV = ATTN_KV_HEADS * HEAD_DIM
ROPE_THETA = 10000.0
ROPE_AXIS_DIM = HEAD_DIM // 2
FFN_HIDDEN = 2816
LOG2_E = 1.4426950408889634

OFF_GV = 0
OFF_GG = 1024
OFF_AQ = 2048
OFF_GATE_A = 3072
OFF_GATE_B = 4096
OFF_GQ = 5120
OFF_GK = 5632
OFF_AK = 6144
OFF_AV = 6400
PROJ_WIDTH = 6656
RANK_PAD = 128

VMEM_LIMIT_BYTES = 56 * 1024 * 1024

NT_DIMS = (((1,), (1,)), ((), ()))
TN_DIMS = (((0,), (0,)), ((), ()))


def _rms(x, g):
    return x * lax.rsqrt(jnp.mean(x * x, axis=-1, keepdims=True) + EPS) * g


def _dot(a, b):
    return jnp.dot(a, b, preferred_element_type=F32)


def _split_bf16(x):
    hi = x.astype(BF16)
    lo = (x - hi.astype(F32)).astype(BF16)
    return hi, lo


def _resident(shape):
    zeros = (0,) * len(shape)
    return pl.BlockSpec(shape, lambda *_: zeros, pipeline_mode=pl.Buffered(1))


def _proj_kernel(x_ref, g_ref, w_ref, wr_ref, o_ref, r_ref, *, tn):
    u = _rms(x_ref[...], g_ref[...]).astype(BF16)
    r_ref[...] = _dot(u, wr_ref[...])
    for n0 in range(0, w_ref.shape[1], tn):
        o_ref[:, n0:n0 + tn] = _dot(u, w_ref[:, n0:n0 + tn]).astype(o_ref.dtype)


def _proj(h, g, w, w_rank, *, tm=512, tn=512):
    t = h.shape[0]
    n = w.shape[1]
    row = lambda i: (i, 0)
    return pl.pallas_call(
        functools.partial(_proj_kernel, tn=tn),
        grid=(t // tm,),
        in_specs=[
            pl.BlockSpec((tm, D_MODEL), row),
            _resident((1, D_MODEL)),
            _resident((D_MODEL, n)),
            _resident((D_MODEL, RANK_PAD)),
        ],
        out_specs=[
            pl.BlockSpec((tm, n), row),
            pl.BlockSpec((tm, RANK_PAD), row),
        ],
        out_shape=[
            jax.ShapeDtypeStruct((t, n), BF16),
            jax.ShapeDtypeStruct((t, RANK_PAD), F32),
        ],
        compiler_params=pltpu.CompilerParams(
            dimension_semantics=("parallel",),
            vmem_limit_bytes=VMEM_LIMIT_BYTES,
        ),
        name="proj",
    )(h, g, w, w_rank)


def _gla_block(direction, q_ref, k_ref, v_ref, gg_ref, r_ref, wup_ref, b_ref, g_ref,
               o_ref, of_ref, st_ref, *, rows, sub, seq_row0):
    c = GLA_CHUNK
    nsub = rows // sub
    nch = sub // c
    forward = direction > 0
    row = lax.broadcasted_iota(jnp.int32, (sub, sub), 0)
    col = lax.broadcasted_iota(jnp.int32, (sub, sub), 1)
    same = (row >> GLA_CHUNK_SHIFT) == (col >> GLA_CHUNK_SHIFT)
    tri = jnp.logical_and(same, (col <= row) if forward else (col >= row))
    sum_mat = jnp.concatenate([jnp.where(tri, 1.0, 0.0), jnp.where(same, 1.0, 0.0)], axis=0).astype(BF16)
    w_hi, w_lo = _split_bf16(wup_ref[...])
    scale = GLA_DK ** -0.5

    for si in (range(nsub) if forward else range(nsub - 1, -1, -1)):
        rs = slice(si * sub, (si + 1) * sub)
        r_hi, r_lo = _split_bf16(r_ref[rs, :])
        x = _dot(r_hi, w_hi) + _dot(r_hi, w_lo) + _dot(r_lo, w_hi) + b_ref[...]
        la = (jnp.minimum(x, 0.0) - jnp.log(1.0 + jnp.exp(-jnp.abs(x)))) * (1.0 / GLA_TAU)
        la_hi, la_lo = _split_bf16(la)
        sums = _dot(sum_mat, la_hi) + _dot(sum_mat, la_lo)
        cum, tot = sums[:sub], sums[sub:]
        e_tot = jnp.exp(tot)
        qd = (q_ref[rs, :].astype(F32) * (scale * jnp.exp(cum))).astype(BF16)
        k_in = k_ref[rs, :].astype(F32) * jnp.exp(-cum)
        k_st = (k_in * e_tot).astype(BF16)
        k_in = k_in.astype(BF16)
        for h in range(GLA_HEADS):
            ks = slice(h * GLA_DK, (h + 1) * GLA_DK)
            vs = slice(h * GLA_DV, (h + 1) * GLA_DV)
            v = v_ref[rs, vs]
            sc = lax.dot_general(qd[:, ks], k_in[:, ks], NT_DIMS, preferred_element_type=F32)
            o = _dot(jnp.where(tri, sc, 0.0).astype(BF16), v)
            st = st_ref[h]
            inter = [None] * nch
            for ci in (range(nch) if forward else range(nch - 1, -1, -1)):
                cr = slice(ci * c, (ci + 1) * c)
                inter[ci] = lax.dot_general(qd[cr, ks], st.astype(BF16), NT_DIMS, preferred_element_type=F32)
                upd = lax.dot_general(v[cr, :], k_st[cr, ks], TN_DIMS, preferred_element_type=F32)
                st = st * e_tot[ci * c:ci * c + 1, ks] + upd
            st_ref[h] = st
            o = o + jnp.concatenate(inter, axis=0)
            seq_rows = pl.ds(pl.multiple_of(seq_row0 + si * sub, sub), sub)
            if forward:
                of_ref[seq_rows, vs] = o
            else:
                y = _rms(of_ref[seq_rows, vs] + o, g_ref[...])
                gg = gg_ref[rs, vs].astype(F32)
                o_ref[rs, vs] = (y * (gg * jax.nn.sigmoid(gg))).astype(o_ref.dtype)


def _gla_kernel(q_ref, k_ref, v_ref, gg_ref, r_ref, wup_ref, b_ref, g_ref,
                o_ref, of_ref, st_ref, *, rows, nblk, sub):
    ph = pl.program_id(1)
    i = pl.program_id(2)

    @pl.when(i == 0)
    def _():
        st_ref[...] = jnp.zeros_like(st_ref)

    refs = (q_ref, k_ref, v_ref, gg_ref, r_ref, wup_ref, b_ref, g_ref, o_ref, of_ref, st_ref)

    @pl.when(ph == 0)
    def _():
        _gla_block(1, *refs, rows=rows, sub=sub, seq_row0=i * rows)

    @pl.when(ph == 1)
    def _():
        _gla_block(-1, *refs, rows=rows, sub=sub, seq_row0=(nblk - 1 - i) * rows)


def _gla(proj, rank, wup_pad, b_alpha, g_out, *, batch, seq, rows=512, sub=256):
    nblk = seq // rows
    t = batch * seq

    def row_block(b, ph, i):
        return b * nblk + jnp.where(ph == 0, i, nblk - 1 - i)

    def parked_block(b, ph, i):
        return b * nblk + jnp.where(ph == 0, nblk - 1, nblk - 1 - i)

    kern = functools.partial(_gla_kernel, rows=rows, nblk=nblk, sub=sub)
    return pl.pallas_call(
        kern,
        grid=(batch, 2, nblk),
        in_specs=[
            pl.BlockSpec((rows, GLA_KEY), lambda b, ph, i: (row_block(b, ph, i), OFF_GQ // GLA_KEY)),
            pl.BlockSpec((rows, GLA_KEY), lambda b, ph, i: (row_block(b, ph, i), OFF_GK // GLA_KEY)),
            pl.BlockSpec((rows, GLA_VAL), lambda b, ph, i: (row_block(b, ph, i), OFF_GV // GLA_VAL)),
            pl.BlockSpec((rows, GLA_VAL), lambda b, ph, i: (parked_block(b, ph, i), OFF_GG // GLA_VAL)),
            pl.BlockSpec((rows, RANK_PAD), lambda b, ph, i: (row_block(b, ph, i), 0)),
            pl.BlockSpec((None, RANK_PAD, GLA_KEY), lambda b, ph, i: (ph, 0, 0)),
            pl.BlockSpec((None, 1, GLA_KEY), lambda b, ph, i: (ph, 0, 0)),
            pl.BlockSpec((1, GLA_DV), lambda b, ph, i: (0, 0)),
        ],
        out_specs=pl.BlockSpec((rows, GLA_VAL), lambda b, ph, i: (parked_block(b, ph, i), 0)),
        out_shape=jax.ShapeDtypeStruct((t, GLA_VAL), BF16),
        scratch_shapes=[
            pltpu.VMEM((seq, GLA_VAL), F32),
            pltpu.VMEM((GLA_HEADS, GLA_DV, GLA_DK), F32),
        ],
        compiler_params=pltpu.CompilerParams(
            dimension_semantics=("parallel", "arbitrary", "arbitrary"),
            vmem_limit_bytes=VMEM_LIMIT_BYTES,
        ),
        name="gla",
    )(proj, proj, proj, proj, rank, wup_pad, b_alpha, g_out)


def _attn_kernel(q_ref, k_ref, v_ref, cos_ref, sa_ref, sb_ref, gq_ref, gk_ref,
                 o_ref, kn_ref, v1_ref, *s_refs, tq, rt, kc):
    qi = pl.program_id(2)
    seq = kn_ref.shape[0]

    def rope(x, rows):
        return (x * cos_ref[rows, :]
                + pltpu.roll(x, HEAD_DIM - 32, 1) * sa_ref[rows, :]
                + pltpu.roll(x, 32, 1) * sb_ref[rows, :])

    @pl.when(qi == 0)
    def _():
        k = _rms(k_ref[...].astype(F32), gk_ref[...])
        kn_ref[...] = rope(k, slice(None)).astype(BF16)
        v1_ref[:, :HEAD_DIM] = v_ref[...]
        v1_ref[:, HEAD_DIM:] = jnp.ones((seq, HEAD_DIM), BF16)

    scale = HEAD_DIM ** -0.5 * LOG2_E

    def pass1(r, hh, s_ref):
        x = _rms(q_ref[r * rt:(r + 1) * rt, hh * HEAD_DIM:(hh + 1) * HEAD_DIM].astype(F32), gq_ref[...])
        rows = pl.ds(pl.multiple_of(qi * tq + r * rt, rt), rt)
        q = (rope(x, rows) * scale).astype(BF16)
        m = jnp.full((rt, HEAD_DIM), -jnp.inf, F32)
        for c0 in range(0, seq, kc):
            s = lax.dot_general(q, kn_ref[c0:c0 + kc, :], NT_DIMS, preferred_element_type=F32)
            s_ref[:, c0:c0 + kc] = s
            for l0 in range(0, kc, HEAD_DIM):
                m = jnp.maximum(m, s[:, l0:l0 + HEAD_DIM])
        return jnp.max(m, axis=-1, keepdims=True)

    def pass2(r, hh, s_ref, m):
        acc = jnp.zeros((rt, 2 * HEAD_DIM), F32)
        for c0 in range(0, seq, kc):
            p = jnp.exp2((s_ref[:, c0:c0 + kc] - m).astype(BF16))
            acc = acc + _dot(p, v1_ref[c0:c0 + kc, :])
        o = acc[:, :HEAD_DIM] / acc[:, HEAD_DIM:HEAD_DIM + 1]
        o_ref[r * rt:(r + 1) * rt, hh * HEAD_DIM:(hh + 1) * HEAD_DIM] = o.astype(o_ref.dtype)

    pending = None
    for t, (r, hh) in enumerate((r, hh) for r in range(tq // rt) for hh in range(ATTN_GROUP)):
        s_ref = s_refs[t % len(s_refs)]
        m = pass1(r, hh, s_ref)
        if pending is not None:
            pass2(*pending)
        pending = (r, hh, s_ref, m)
    pass2(*pending)


def _attn(proj, cos_t, sa_t, sb_t, g_q, g_k, *, batch, seq, tq=512, rt=128, kc=512):
    nq = seq // tq
    t = batch * seq
    gw = ATTN_GROUP * HEAD_DIM
    kern = functools.partial(_attn_kernel, tq=tq, rt=rt, kc=kc)
    return pl.pallas_call(
        kern,
        grid=(batch, ATTN_KV_HEADS, nq),
        in_specs=[
            pl.BlockSpec((tq, gw), lambda b, g, i: (b * nq + i, OFF_AQ // gw + g)),
            pl.BlockSpec((seq, HEAD_DIM), lambda b, g, i: (b, OFF_AK // HEAD_DIM + g)),
            pl.BlockSpec((seq, HEAD_DIM), lambda b, g, i: (b, OFF_AV // HEAD_DIM + g)),
            _resident((seq, HEAD_DIM)),
            _resident((seq, HEAD_DIM)),
            _resident((seq, HEAD_DIM)),
            pl.BlockSpec((1, HEAD_DIM), lambda b, g, i: (0, 0)),
            pl.BlockSpec((1, HEAD_DIM), lambda b, g, i: (0, 0)),
        ],
        out_specs=pl.BlockSpec((tq, gw), lambda b, g, i: (b * nq + i, g)),
        out_shape=jax.ShapeDtypeStruct((t, ATTN_Q), BF16),
        scratch_shapes=[
            pltpu.VMEM((seq, HEAD_DIM), BF16),
            pltpu.VMEM((seq, 2 * HEAD_DIM), BF16),
            pltpu.VMEM((rt, seq), F32),
            pltpu.VMEM((rt, seq), F32),
        ],
        compiler_params=pltpu.CompilerParams(
            dimension_semantics=("parallel", "parallel", "arbitrary"),
            vmem_limit_bytes=VMEM_LIMIT_BYTES,
        ),
        name="attn",
    )(proj, proj, proj, cos_t, sa_t, sb_t, g_q, g_k)


def _merge_kernel(a_ref, at_ref, ga_ref, gb_ref, h_ref, woa_ref, wob_ref, wout_ref, g_ref, o_ref):
    branch_a = _dot(a_ref[...], woa_ref[...])
    branch_b = _dot(at_ref[...], wob_ref[...])
    mixed = (jax.nn.sigmoid(ga_ref[...].astype(F32)) * branch_a
             + jax.nn.sigmoid(gb_ref[...].astype(F32)) * branch_b)
    y = _dot(mixed.astype(BF16), wout_ref[...])
    o_ref[...] = h_ref[...] + _rms(y, g_ref[...])


def _merge(gla_out, attn_out, proj, h, w_o_gla, w_o_attn, w_out, g_post, *, tm=512):
    t = h.shape[0]
    row = lambda i: (i, 0)
    return pl.pallas_call(
        _merge_kernel,
        grid=(t // tm,),
        in_specs=[
            pl.BlockSpec((tm, GLA_VAL), row),
            pl.BlockSpec((tm, ATTN_Q), row),
            pl.BlockSpec((tm, D_MODEL), lambda i: (i, OFF_GATE_A // D_MODEL)),
            pl.BlockSpec((tm, D_MODEL), lambda i: (i, OFF_GATE_B // D_MODEL)),
            pl.BlockSpec((tm, D_MODEL), row),
            _resident((GLA_VAL, D_MODEL)),
            _resident((ATTN_Q, D_MODEL)),
            _resident((D_MODEL, D_MODEL)),
            _resident((1, D_MODEL)),
        ],
        out_specs=pl.BlockSpec((tm, D_MODEL), row),
        out_shape=jax.ShapeDtypeStruct((t, D_MODEL), F32),
        compiler_params=pltpu.CompilerParams(
            dimension_semantics=("parallel",),
            vmem_limit_bytes=VMEM_LIMIT_BYTES,
        ),
        name="merge",
    )(gla_out, attn_out, proj, proj, h, w_o_gla, w_o_attn, w_out, g_post)


def _ffn_kernel(h_ref, p_ref, gpre_ref, win_ref, wout_ref, gpost_ref, wpp_ref, wpg_ref, gple_ref,
                o_ref, *, hidden_chunk):
    h = h_ref[...]
    x = _rms(h, gpre_ref[...]).astype(BF16)
    acc = jnp.zeros(h.shape, F32)
    for c0 in range(0, FFN_HIDDEN, hidden_chunk):
        gate = _dot(x, win_ref[:, c0:c0 + hidden_chunk])
        up = _dot(x, win_ref[:, FFN_HIDDEN + c0:FFN_HIDDEN + c0 + hidden_chunk])
        act = (gate * jax.nn.sigmoid(gate)) * up
        acc = acc + _dot(act.astype(BF16), wout_ref[c0:c0 + hidden_chunk, :])
    h = h + _rms(acc, gpost_ref[...])
    e = _dot(p_ref[...].astype(BF16), wpp_ref[...])
    gate = jax.nn.sigmoid(_dot(h.astype(BF16), wpg_ref[...]))
    o_ref[...] = h + _rms(gate * e, gple_ref[...])


def _ffn(h, p_i, g_pre, w_in, w_out, g_post, w_pp, w_pg, g_ple, *, tm=512, hidden_chunk=1408):
    t = h.shape[0]
    row = lambda i: (i, 0)
    kern = functools.partial(_ffn_kernel, hidden_chunk=hidden_chunk)
    return pl.pallas_call(
        kern,
        grid=(t // tm,),
        in_specs=[
            pl.BlockSpec((tm, D_MODEL), row),
            pl.BlockSpec((tm, PLE_DIM), row),
            _resident((1, D_MODEL)),
            _resident((D_MODEL, 2 * FFN_HIDDEN)),
            _resident((FFN_HIDDEN, D_MODEL)),
            _resident((1, D_MODEL)),
            _resident((PLE_DIM, D_MODEL)),
            _resident((D_MODEL, D_MODEL)),
            _resident((1, D_MODEL)),
        ],
        out_specs=pl.BlockSpec((tm, D_MODEL), row),
        out_shape=jax.ShapeDtypeStruct((t, D_MODEL), F32),
        compiler_params=pltpu.CompilerParams(
            dimension_semantics=("parallel",),
            vmem_limit_bytes=VMEM_LIMIT_BYTES,
        ),
        name="ffn",
    )(h, p_i, g_pre, w_in, w_out, g_post, w_pp, w_pg, g_ple)


def _rope_tables(seq):
    rows = seq // GRID_W
    row = jnp.repeat(jnp.arange(rows, dtype=F32), GRID_W)
    col = jnp.tile(jnp.arange(GRID_W, dtype=F32), rows)
    inv = ROPE_THETA ** (-jnp.arange(0, ROPE_AXIS_DIM, 2, dtype=F32) / ROPE_AXIS_DIM)
    ang = jnp.stack([row[:, None] * inv, col[:, None] * inv], axis=1)
    cos, sin = jnp.cos(ang), jnp.sin(ang)
    zero = jnp.zeros_like(sin)
    lanes = lambda a, b: jnp.stack([a, b], axis=2).reshape(seq, HEAD_DIM)
    return lanes(cos, cos), lanes(-sin, zero), lanes(zero, sin)


def _permute_w_in(w_in):
    sizes = (GLA_KEY, GLA_KEY, GLA_VAL, GLA_VAL, GLA_RANK, GLA_RANK,
             ATTN_Q, ATTN_KV, ATTN_KV, D_MODEL, D_MODEL)
    offs = np.cumsum((0,) + sizes)
    gq, gk, gv, gg, ra_f, ra_b, aq, ak, av, gate_a, gate_b = (
        w_in[:, offs[n]:offs[n + 1]] for n in range(len(sizes)))
    main = jnp.concatenate([gv, gg, aq, gate_a, gate_b, gq, gk, ak, av], axis=1)
    rank = jnp.concatenate([ra_f, ra_b], axis=1)
    rank = jnp.pad(rank, ((0, 0), (0, RANK_PAD - 2 * GLA_RANK)))
    return main.astype(BF16), rank.astype(BF16)


def _pad_w_alpha(w_alpha_up):
    out = jnp.zeros((2, RANK_PAD, GLA_KEY), F32)
    for d in range(2):
        out = out.at[d, d * GLA_RANK:(d + 1) * GLA_RANK, :].set(w_alpha_up[d])
    return out


def kernel(x, p, g_mix_pre, w_in, w_alpha_up, b_alpha, g_gla_out, g_q_norm, g_k_norm, w_o_gla,
           w_o_attn, w_out, g_mix_post, g_ffn_pre, w_ffn_in, w_ffn_out, g_ffn_post, w_ple_proj,
           w_ple_gate, g_ple_post):
    batch, seq, _ = x.shape
    depth = p.shape[0]
    t = batch * seq
    cos_t, sa_t, sb_t = _rope_tables(seq)
    h = x.reshape(t, D_MODEL)
    row = lambda v: v.reshape(1, -1)
    for i in range(depth):
        w_main, w_rank = _permute_w_in(w_in[i])
        proj, rank = _proj(h, row(g_mix_pre[i]), w_main, w_rank)
        gla_out = _gla(proj, rank, _pad_w_alpha(w_alpha_up[i]), b_alpha[i].reshape(2, 1, GLA_KEY),
                       row(g_gla_out[i]), batch=batch, seq=seq)
        attn_out = _attn(proj, cos_t, sa_t, sb_t, row(g_q_norm[i]), row(g_k_norm[i]),
                         batch=batch, seq=seq)
        h = _merge(gla_out, attn_out, proj, h, w_o_gla[i].astype(BF16), w_o_attn[i].astype(BF16),
                   w_out[i].astype(BF16), row(g_mix_post[i]))
        h = _ffn(h, p[i].reshape(t, PLE_DIM), row(g_ffn_pre[i]), w_ffn_in[i].astype(BF16),
                 w_ffn_out[i].astype(BF16), row(g_ffn_post[i]), w_ple_proj[i].astype(BF16),
                 w_ple_gate[i].astype(BF16), row(g_ple_post[i]))
    return h.reshape(batch, seq, D_MODEL)
```

```python
import functools

import jax
import jax.numpy as jnp
import numpy as np
from jax import lax
from jax.experimental import pallas as pl
from jax.experimental.pallas import tpu as pltpu

F32 = jnp.float32
BF16 = jnp.bfloat16

D_MODEL = 1024
GRID_W = 64
PLE_DIM = 256
EPS = 1e-6
GLA_HEADS = 4
GLA_DK = 128
GLA_DV = 256
GLA_KEY = GLA_HEADS * GLA_DK
GLA_VAL = GLA_HEADS * GLA_DV
GLA_RANK = 16
GLA_TAU = 16.0
GLA_CHUNK = 64
GLA_CHUNK_SHIFT = 6
assert 1 << GLA_CHUNK_SHIFT == GLA_CHUNK
ATTN_HEADS = 8
ATTN_KV_HEADS = 2
ATTN_GROUP = ATTN_HEADS // ATTN_KV_HEADS
HEAD_DIM = 128
ATTN_Q = ATTN_HEADS * HEAD_DIM
ATTN_KV = ATTN_KV_HEADS * HEAD_DIM
ROPE_THETA = 10000.0
ROPE_AXIS_DIM = HEAD_DIM // 2
FFN_HIDDEN = 2816
LOG2_E = 1.4426950408889634

OFF_GV = 0
OFF_GG = 1024
OFF_AQ = 2048
OFF_GATE_A = 3072
OFF_GATE_B = 4096
OFF_GQ = 5120
OFF_GK = 5632
OFF_AK = 6144
OFF_AV = 6400
PROJ_WIDTH = 6656
RANK_PAD = 128

VMEM_LIMIT_BYTES = 56 * 1024 * 1024

NT_DIMS = (((1,), (1,)), ((), ()))
TN_DIMS = (((0,), (0,)), ((), ()))


def _rms(x, g):
    return x * lax.rsqrt(jnp.mean(x * x, axis=-1, keepdims=True) + EPS) * g


def _dot(a, b):
    return jnp.dot(a, b, preferred_element_type=F32)


def _resident(shape):
    zeros = (0,) * len(shape)
    return pl.BlockSpec(shape, lambda *_: zeros, pipeline_mode=pl.Buffered(1))


def _proj_kernel(x_ref, g_ref, w_ref, wr_ref, o_ref, r_ref, *, tn):
    u = _rms(x_ref[...], g_ref[...]).astype(BF16)
    r_ref[...] = _dot(u, wr_ref[...])
    for n0 in range(0, w_ref.shape[1], tn):
        o_ref[:, n0:n0 + tn] = _dot(u, w_ref[:, n0:n0 + tn]).astype(o_ref.dtype)


def _proj(h, g, w, w_rank, *, tm=512, tn=512):
    t = h.shape[0]
    n = w.shape[1]
    row = lambda i: (i, 0)
    return pl.pallas_call(
        functools.partial(_proj_kernel, tn=tn),
        grid=(t // tm,),
        in_specs=[
            pl.BlockSpec((tm, D_MODEL), row),
            _resident((1, D_MODEL)),
            _resident((D_MODEL, n)),
            _resident((D_MODEL, RANK_PAD)),
        ],
        out_specs=[
            pl.BlockSpec((tm, n), row),
            pl.BlockSpec((tm, RANK_PAD), row),
        ],
        out_shape=[
            jax.ShapeDtypeStruct((t, n), BF16),
            jax.ShapeDtypeStruct((t, RANK_PAD), F32),
        ],
        compiler_params=pltpu.CompilerParams(
            dimension_semantics=("parallel",),
            vmem_limit_bytes=VMEM_LIMIT_BYTES,
        ),
        name="proj",
    )(h, g, w, w_rank)


def _gla_block(direction, q_ref, k_ref, v_ref, gg_ref, r_ref, wup_ref, b_ref, g_ref,
               o_ref, of_ref, st_ref, *, rows, sub, seq_row0):
    c = GLA_CHUNK
    nsub = rows // sub
    nch = sub // c
    forward = direction > 0
    row = lax.broadcasted_iota(jnp.int32, (sub, sub), 0)
    col = lax.broadcasted_iota(jnp.int32, (sub, sub), 1)
    same = (row >> GLA_CHUNK_SHIFT) == (col >> GLA_CHUNK_SHIFT)
    tri = jnp.logical_and(same, (col <= row) if forward else (col >= row))
    sum_mat = jnp.concatenate([jnp.where(tri, 1.0, 0.0), jnp.where(same, 1.0, 0.0)], axis=0).astype(BF16)
    w_up = wup_ref[...].astype(BF16)
    scale = GLA_DK ** -0.5

    for si in (range(nsub) if forward else range(nsub - 1, -1, -1)):
        rs = slice(si * sub, (si + 1) * sub)
        x = _dot(r_ref[rs, :].astype(BF16), w_up) + b_ref[...]
        la = (jnp.minimum(x, 0.0) - jnp.log(1.0 + jnp.exp(-jnp.abs(x)))) * (1.0 / GLA_TAU)
        sums = _dot(sum_mat, la.astype(BF16))
        cum, tot = sums[:sub], sums[sub:]
        e_tot = jnp.exp(tot)
        qd = (q_ref[rs, :].astype(F32) * (scale * jnp.exp(cum))).astype(BF16)
        k_in = k_ref[rs, :].astype(F32) * jnp.exp(-cum)
        k_st = (k_in * e_tot).astype(BF16)
        k_in = k_in.astype(BF16)
        for h in range(GLA_HEADS):
            ks = slice(h * GLA_DK, (h + 1) * GLA_DK)
            vs = slice(h * GLA_DV, (h + 1) * GLA_DV)
            v = v_ref[rs, vs]
            sc = lax.dot_general(qd[:, ks], k_in[:, ks], NT_DIMS, preferred_element_type=F32)
            o = _dot(jnp.where(tri, sc, 0.0).astype(BF16), v)
            st = st_ref[h]
            inter = [None] * nch
            for ci in (range(nch) if forward else range(nch - 1, -1, -1)):
                cr = slice(ci * c, (ci + 1) * c)
                inter[ci] = lax.dot_general(qd[cr, ks], st.astype(BF16), NT_DIMS, preferred_element_type=F32)
                upd = lax.dot_general(v[cr, :], k_st[cr, ks], TN_DIMS, preferred_element_type=F32)
                st = st * e_tot[ci * c:ci * c + 1, ks] + upd
            st_ref[h] = st
            o = o + jnp.concatenate(inter, axis=0)
            seq_rows = pl.ds(pl.multiple_of(seq_row0 + si * sub, sub), sub)
            if forward:
                of_ref[seq_rows, vs] = o
            else:
                y = _rms(of_ref[seq_rows, vs] + o, g_ref[...])
                gg = gg_ref[rs, vs].astype(F32)
                o_ref[rs, vs] = (y * (gg * jax.nn.sigmoid(gg))).astype(o_ref.dtype)


def _gla_kernel(q_ref, k_ref, v_ref, gg_ref, r_ref, wup_ref, b_ref, g_ref,
                o_ref, of_ref, st_ref, *, rows, nblk, sub):
    ph = pl.program_id(1)
    i = pl.program_id(2)

    @pl.when(i == 0)
    def _():
        st_ref[...] = jnp.zeros_like(st_ref)

    refs = (q_ref, k_ref, v_ref, gg_ref, r_ref, wup_ref, b_ref, g_ref, o_ref, of_ref, st_ref)

    @pl.when(ph == 0)
    def _():
        _gla_block(1, *refs, rows=rows, sub=sub, seq_row0=i * rows)

    @pl.when(ph == 1)
    def _():
        _gla_block(-1, *refs, rows=rows, sub=sub, seq_row0=(nblk - 1 - i) * rows)


def _gla(proj, rank, wup_pad, b_alpha, g_out, *, batch, seq, rows=512, sub=256):
    nblk = seq // rows
    t = batch * seq

    def row_block(b, ph, i):
        return b * nblk + jnp.where(ph == 0, i, nblk - 1 - i)

    def parked_block(b, ph, i):
        return b * nblk + jnp.where(ph == 0, nblk - 1, nblk - 1 - i)

    kern = functools.partial(_gla_kernel, rows=rows, nblk=nblk, sub=sub)
    return pl.pallas_call(
        kern,
        grid=(batch, 2, nblk),
        in_specs=[
            pl.BlockSpec((rows, GLA_KEY), lambda b, ph, i: (row_block(b, ph, i), OFF_GQ // GLA_KEY)),
            pl.BlockSpec((rows, GLA_KEY), lambda b, ph, i: (row_block(b, ph, i), OFF_GK // GLA_KEY)),
            pl.BlockSpec((rows, GLA_VAL), lambda b, ph, i: (row_block(b, ph, i), OFF_GV // GLA_VAL)),
            pl.BlockSpec((rows, GLA_VAL), lambda b, ph, i: (parked_block(b, ph, i), OFF_GG // GLA_VAL)),
            pl.BlockSpec((rows, RANK_PAD), lambda b, ph, i: (row_block(b, ph, i), 0)),
            pl.BlockSpec((None, RANK_PAD, GLA_KEY), lambda b, ph, i: (ph, 0, 0)),
            pl.BlockSpec((None, 1, GLA_KEY), lambda b, ph, i: (ph, 0, 0)),
            pl.BlockSpec((1, GLA_DV), lambda b, ph, i: (0, 0)),
        ],
        out_specs=pl.BlockSpec((rows, GLA_VAL), lambda b, ph, i: (parked_block(b, ph, i), 0)),
        out_shape=jax.ShapeDtypeStruct((t, GLA_VAL), BF16),
        scratch_shapes=[
            pltpu.VMEM((seq, GLA_VAL), F32),
            pltpu.VMEM((GLA_HEADS, GLA_DV, GLA_DK), F32),
        ],
        compiler_params=pltpu.CompilerParams(
            dimension_semantics=("parallel", "arbitrary", "arbitrary"),
            vmem_limit_bytes=VMEM_LIMIT_BYTES,
        ),
        name="gla",
    )(proj, proj, proj, proj, rank, wup_pad, b_alpha, g_out)


def _attn_kernel(q_ref, k_ref, v_ref, cos_ref, sa_ref, sb_ref, gq_ref, gk_ref,
                 o_ref, kn_ref, v1_ref, *s_refs, tq, rt, kc):
    qi = pl.program_id(2)
    seq = kn_ref.shape[0]

    def rope(x, rows):
        return (x * cos_ref[rows, :]
                + pltpu.roll(x, HEAD_DIM - 32, 1) * sa_ref[rows, :]
                + pltpu.roll(x, 32, 1) * sb_ref[rows, :])

    @pl.when(qi == 0)
    def _():
        k = _rms(k_ref[...].astype(F32), gk_ref[...])
        kn_ref[...] = rope(k, slice(None)).astype(BF16)
        v1_ref[:, :HEAD_DIM] = v_ref[...]
        v1_ref[:, HEAD_DIM:] = jnp.ones((seq, HEAD_DIM), BF16)

    scale = HEAD_DIM ** -0.5 * LOG2_E

    def pass1(r, hh, s_ref):
        x = _rms(q_ref[r * rt:(r + 1) * rt, hh * HEAD_DIM:(hh + 1) * HEAD_DIM].astype(F32), gq_ref[...])
        rows = pl.ds(pl.multiple_of(qi * tq + r * rt, rt), rt)
        q = (rope(x, rows) * scale).astype(BF16)
        m = jnp.full((rt, HEAD_DIM), -jnp.inf, F32)
        for c0 in range(0, seq, kc):
            s = lax.dot_general(q, kn_ref[c0:c0 + kc, :], NT_DIMS, preferred_element_type=F32)
            s_ref[:, c0:c0 + kc] = s
            for l0 in range(0, kc, HEAD_DIM):
                m = jnp.maximum(m, s[:, l0:l0 + HEAD_DIM])
        return jnp.max(m, axis=-1, keepdims=True)

    def pass2(r, hh, s_ref, m):
        acc = jnp.zeros((rt, 2 * HEAD_DIM), F32)
        for c0 in range(0, seq, kc):
            p = jnp.exp2((s_ref[:, c0:c0 + kc] - m).astype(BF16))
            acc = acc + _dot(p, v1_ref[c0:c0 + kc, :])
        o = acc[:, :HEAD_DIM] / acc[:, HEAD_DIM:HEAD_DIM + 1]
        o_ref[r * rt:(r + 1) * rt, hh * HEAD_DIM:(hh + 1) * HEAD_DIM] = o.astype(o_ref.dtype)

    pending = None
    for t, (r, hh) in enumerate((r, hh) for r in range(tq // rt) for hh in range(ATTN_GROUP)):
        s_ref = s_refs[t % len(s_refs)]
        m = pass1(r, hh, s_ref)
        if pending is not None:
            pass2(*pending)
        pending = (r, hh, s_ref, m)
    pass2(*pending)


def _attn(proj, cos_t, sa_t, sb_t, g_q, g_k, *, batch, seq, tq=512, rt=128, kc=512):
    nq = seq // tq
    t = batch * seq
    gw = ATTN_GROUP * HEAD_DIM
    kern = functools.partial(_attn_kernel, tq=tq, rt=rt, kc=kc)
    return pl.pallas_call(
        kern,
        grid=(batch, ATTN_KV_HEADS, nq),
        in_specs=[
            pl.BlockSpec((tq, gw), lambda b, g, i: (b * nq + i, OFF_AQ // gw + g)),
            pl.BlockSpec((seq, HEAD_DIM), lambda b, g, i: (b, OFF_AK // HEAD_DIM + g)),
            pl.BlockSpec((seq, HEAD_DIM), lambda b, g, i: (b, OFF_AV // HEAD_DIM + g)),
            _resident((seq, HEAD_DIM)),
            _resident((seq, HEAD_DIM)),
            _resident((seq, HEAD_DIM)),
            pl.BlockSpec((1, HEAD_DIM), lambda b, g, i: (0, 0)),
            pl.BlockSpec((1, HEAD_DIM), lambda b, g, i: (0, 0)),
        ],
        out_specs=pl.BlockSpec((tq, gw), lambda b, g, i: (b * nq + i, g)),
        out_shape=jax.ShapeDtypeStruct((t, ATTN_Q), BF16),
        scratch_shapes=[
            pltpu.VMEM((seq, HEAD_DIM), BF16),
            pltpu.VMEM((seq, 2 * HEAD_DIM), BF16),
            pltpu.VMEM((rt, seq), F32),
            pltpu.VMEM((rt, seq), F32),
        ],
        compiler_params=pltpu.CompilerParams(
            dimension_semantics=("parallel", "parallel", "arbitrary"),
            vmem_limit_bytes=VMEM_LIMIT_BYTES,
        ),
        name="attn",
    )(proj, proj, proj, cos_t, sa_t, sb_t, g_q, g_k)


def _merge_kernel(a_ref, at_ref, ga_ref, gb_ref, h_ref, woa_ref, wob_ref, wout_ref, g_ref, o_ref,
                  *, sub):
    for r0 in range(0, o_ref.shape[0], sub):
        rs = slice(r0, r0 + sub)
        branch_a = _dot(a_ref[rs, :], woa_ref[...])
        branch_b = _dot(at_ref[rs, :], wob_ref[...])
        mixed = (jax.nn.sigmoid(ga_ref[rs, :].astype(F32)) * branch_a
                 + jax.nn.sigmoid(gb_ref[rs, :].astype(F32)) * branch_b)
        y = _dot(mixed.astype(BF16), wout_ref[...])
        o_ref[rs, :] = h_ref[rs, :] + _rms(y, g_ref[...])


def _merge(gla_out, attn_out, proj, h, w_o_gla, w_o_attn, w_out, g_post, *, tm=512, sub=256):
    t = h.shape[0]
    row = lambda i: (i, 0)
    return pl.pallas_call(
        functools.partial(_merge_kernel, sub=sub),
        grid=(t // tm,),
        in_specs=[
            pl.BlockSpec((tm, GLA_VAL), row),
            pl.BlockSpec((tm, ATTN_Q), row),
            pl.BlockSpec((tm, D_MODEL), lambda i: (i, OFF_GATE_A // D_MODEL)),
            pl.BlockSpec((tm, D_MODEL), lambda i: (i, OFF_GATE_B // D_MODEL)),
            pl.BlockSpec((tm, D_MODEL), row),
            _resident((GLA_VAL, D_MODEL)),
            _resident((ATTN_Q, D_MODEL)),
            _resident((D_MODEL, D_MODEL)),
            _resident((1, D_MODEL)),
        ],
        out_specs=pl.BlockSpec((tm, D_MODEL), row),
        out_shape=jax.ShapeDtypeStruct((t, D_MODEL), F32),
        compiler_params=pltpu.CompilerParams(
            dimension_semantics=("parallel",),
            vmem_limit_bytes=VMEM_LIMIT_BYTES,
        ),
        name="merge",
    )(gla_out, attn_out, proj, proj, h, w_o_gla, w_o_attn, w_out, g_post)


def _ffn_kernel(h_ref, p_ref, gpre_ref, win_ref, wout_ref, gpost_ref, wpp_ref, wpg_ref, gple_ref,
                o_ref, *, hidden_chunk, sub):
    for r0 in range(0, o_ref.shape[0], sub):
        rs = slice(r0, r0 + sub)
        h = h_ref[rs, :]
        x = _rms(h, gpre_ref[...]).astype(BF16)
        acc = jnp.zeros(h.shape, F32)
        for c0 in range(0, FFN_HIDDEN, hidden_chunk):
            gate = _dot(x, win_ref[:, c0:c0 + hidden_chunk])
            up = _dot(x, win_ref[:, FFN_HIDDEN + c0:FFN_HIDDEN + c0 + hidden_chunk])
            act = (gate * jax.nn.sigmoid(gate)) * up
            acc = acc + _dot(act.astype(BF16), wout_ref[c0:c0 + hidden_chunk, :])
        h = h + _rms(acc, gpost_ref[...])
        e = _dot(p_ref[rs, :].astype(BF16), wpp_ref[...])
        gate = jax.nn.sigmoid(_dot(h.astype(BF16), wpg_ref[...]))
        o_ref[rs, :] = h + _rms(gate * e, gple_ref[...])


def _ffn(h, p_i, g_pre, w_in, w_out, g_post, w_pp, w_pg, g_ple, *, tm=512, sub=256, hidden_chunk=FFN_HIDDEN):
    t = h.shape[0]
    row = lambda i: (i, 0)
    kern = functools.partial(_ffn_kernel, hidden_chunk=hidden_chunk, sub=sub)
    return pl.pallas_call(
        kern,
        grid=(t // tm,),
        in_specs=[
            pl.BlockSpec((tm, D_MODEL), row),
            pl.BlockSpec((tm, PLE_DIM), row),
            _resident((1, D_MODEL)),
            _resident((D_MODEL, 2 * FFN_HIDDEN)),
            _resident((FFN_HIDDEN, D_MODEL)),
            _resident((1, D_MODEL)),
            _resident((PLE_DIM, D_MODEL)),
            _resident((D_MODEL, D_MODEL)),
            _resident((1, D_MODEL)),
        ],
        out_specs=pl.BlockSpec((tm, D_MODEL), row),
        out_shape=jax.ShapeDtypeStruct((t, D_MODEL), F32),
        compiler_params=pltpu.CompilerParams(
            dimension_semantics=("parallel",),
            vmem_limit_bytes=VMEM_LIMIT_BYTES,
        ),
        name="ffn",
    )(h, p_i, g_pre, w_in, w_out, g_post, w_pp, w_pg, g_ple)


def _rope_tables(seq):
    rows = seq // GRID_W
    row = jnp.repeat(jnp.arange(rows, dtype=F32), GRID_W)
    col = jnp.tile(jnp.arange(GRID_W, dtype=F32), rows)
    inv = ROPE_THETA ** (-jnp.arange(0, ROPE_AXIS_DIM, 2, dtype=F32) / ROPE_AXIS_DIM)
    ang = jnp.stack([row[:, None] * inv, col[:, None] * inv], axis=1)
    cos, sin = jnp.cos(ang), jnp.sin(ang)
    zero = jnp.zeros_like(sin)
    lanes = lambda a, b: jnp.stack([a, b], axis=2).reshape(seq, HEAD_DIM)
    return lanes(cos, cos), lanes(-sin, zero), lanes(zero, sin)


def _permute_w_in(w_in):
    sizes = (GLA_KEY, GLA_KEY, GLA_VAL, GLA_VAL, GLA_RANK, GLA_RANK,
             ATTN_Q, ATTN_KV, ATTN_KV, D_MODEL, D_MODEL)
    offs = np.cumsum((0,) + sizes)
    gq, gk, gv, gg, ra_f, ra_b, aq, ak, av, gate_a, gate_b = (
        w_in[:, offs[n]:offs[n + 1]] for n in range(len(sizes)))
    main = jnp.concatenate([gv, gg, aq, gate_a, gate_b, gq, gk, ak, av], axis=1)
    rank = jnp.concatenate([ra_f, ra_b], axis=1)
    rank = jnp.pad(rank, ((0, 0), (0, RANK_PAD - 2 * GLA_RANK)))
    return main.astype(BF16), rank.astype(BF16)


def _pad_w_alpha(w_alpha_up):
    out = jnp.zeros((2, RANK_PAD, GLA_KEY), F32)
    for d in range(2):
        out = out.at[d, d * GLA_RANK:(d + 1) * GLA_RANK, :].set(w_alpha_up[d])
    return out


def kernel(x, p, g_mix_pre, w_in, w_alpha_up, b_alpha, g_gla_out, g_q_norm, g_k_norm, w_o_gla,
           w_o_attn, w_out, g_mix_post, g_ffn_pre, w_ffn_in, w_ffn_out, g_ffn_post, w_ple_proj,
           w_ple_gate, g_ple_post):
    batch, seq, _ = x.shape
    depth = p.shape[0]
    t = batch * seq
    cos_t, sa_t, sb_t = _rope_tables(seq)
    h = x.reshape(t, D_MODEL)
    row = lambda v: v.reshape(1, -1)
    for i in range(depth):
        w_main, w_rank = _permute_w_in(w_in[i])
        proj, rank = _proj(h, row(g_mix_pre[i]), w_main, w_rank)
        gla_out = _gla(proj, rank, _pad_w_alpha(w_alpha_up[i]), b_alpha[i].reshape(2, 1, GLA_KEY),
                       row(g_gla_out[i]), batch=batch, seq=seq)
        attn_out = _attn(proj, cos_t, sa_t, sb_t, row(g_q_norm[i]), row(g_k_norm[i]),
                         batch=batch, seq=seq)
        h = _merge(gla_out, attn_out, proj, h, w_o_gla[i].astype(BF16), w_o_attn[i].astype(BF16),
                   w_out[i].astype(BF16), row(g_mix_post[i]))
        h = _ffn(h, p[i].reshape(t, PLE_DIM), row(g_ffn_pre[i]), w_ffn_in[i].astype(BF16),
                 w_ffn_out[i].astype(BF16), row(g_ffn_post[i]), w_ple_proj[i].astype(BF16),
                 w_ple_gate[i].astype(BF16), row(g_ple_post[i]))
    return h.reshape(batch, seq, D_MODEL)
```

```python
import functools

import jax
import jax.numpy as jnp
import numpy as np
from jax import lax
from jax.experimental import pallas as pl
from jax.experimental.pallas import tpu as pltpu

F32 = jnp.float32
BF16 = jnp.bfloat16

D_MODEL = 1024
GRID_W = 64
PLE_DIM = 256
EPS = 1e-6
GLA_HEADS = 4
GLA_DK = 128
GLA_DV = 256
GLA_KEY = GLA_HEADS * GLA_DK
GLA_VAL = GLA_HEADS * GLA_DV
GLA_RANK = 16
GLA_TAU = 16.0
GLA_CHUNK = 64
GLA_CHUNK_SHIFT = 6
assert 1 << GLA_CHUNK_SHIFT == GLA_CHUNK
ATTN_HEADS = 8
ATTN_KV_HEADS = 2
ATTN_GROUP = ATTN_HEADS // ATTN_KV_HEADS
HEAD_DIM = 128
ATTN_Q = ATTN_HEADS * HEAD_DIM
ATTN_KV = ATTN_KV_HEADS * HEAD_DIM
ROPE_THETA = 10000.0
ROPE_AXIS_DIM = HEAD_DIM // 2
FFN_HIDDEN = 2816
LOG2_E = 1.4426950408889634

OFF_GV = 0
OFF_GG = 1024
OFF_AQ = 2048
OFF_GATE_A = 3072
OFF_GATE_B = 4096
OFF_GQ = 5120
OFF_GK = 5632
OFF_AK = 6144
OFF_AV = 6400
PROJ_WIDTH = 6656
RANK_PAD = 128

VMEM_LIMIT_BYTES = 56 * 1024 * 1024

NT_DIMS = (((1,), (1,)), ((), ()))
TN_DIMS = (((0,), (0,)), ((), ()))


def _rms(x, g):
    return x * lax.rsqrt(jnp.mean(x * x, axis=-1, keepdims=True) + EPS) * g


def _dot(a, b):
    return jnp.dot(a, b, preferred_element_type=F32)


def _resident(shape):
    zeros = (0,) * len(shape)
    return pl.BlockSpec(shape, lambda *_: zeros, pipeline_mode=pl.Buffered(1))


def _layer_resident(layer, shape):
    index = (layer,) + (0,) * len(shape)
    return pl.BlockSpec((None,) + shape, lambda *_: index, pipeline_mode=pl.Buffered(1))


def _proj_kernel(x_ref, g_ref, w_ref, wr_ref, o_ref, r_ref, *, tn):
    u = _rms(x_ref[...], g_ref[...]).astype(BF16)
    r_ref[...] = _dot(u, wr_ref[...])
    for n0 in range(0, w_ref.shape[1], tn):
        o_ref[:, n0:n0 + tn] = _dot(u, w_ref[:, n0:n0 + tn]).astype(o_ref.dtype)


def _proj(layer, h, g, w, w_rank, *, tm=512, tn=512):
    t = h.shape[0]
    n = w.shape[2]
    row = lambda i: (i, 0)
    return pl.pallas_call(
        functools.partial(_proj_kernel, tn=tn),
        grid=(t // tm,),
        in_specs=[
            pl.BlockSpec((tm, D_MODEL), row),
            _layer_resident(layer, (1, D_MODEL)),
            _layer_resident(layer, (D_MODEL, n)),
            _layer_resident(layer, (D_MODEL, RANK_PAD)),
        ],
        out_specs=[
            pl.BlockSpec((tm, n), row),
            pl.BlockSpec((tm, RANK_PAD), row),
        ],
        out_shape=[
            jax.ShapeDtypeStruct((t, n), BF16),
            jax.ShapeDtypeStruct((t, RANK_PAD), F32),
        ],
        compiler_params=pltpu.CompilerParams(
            dimension_semantics=("parallel",),
            vmem_limit_bytes=VMEM_LIMIT_BYTES,
        ),
        name="proj",
    )(h, g, w, w_rank)


def _gla_block(direction, q_ref, k_ref, v_ref, gg_ref, r_ref, wup_ref, b_ref, g_ref,
               o_ref, of_ref, st_ref, qd_ref, kin_ref, kst_ref, et_ref, *, rows, sub, seq_row0):
    c = GLA_CHUNK
    nsub = rows // sub
    nch = sub // c
    forward = direction > 0
    row = lax.broadcasted_iota(jnp.int32, (sub, sub), 0)
    col = lax.broadcasted_iota(jnp.int32, (sub, sub), 1)
    same = (row >> GLA_CHUNK_SHIFT) == (col >> GLA_CHUNK_SHIFT)
    tri = jnp.logical_and(same, (col <= row) if forward else (col >= row))
    sum_mat = jnp.concatenate([jnp.where(tri, 1.0, 0.0), jnp.where(same, 1.0, 0.0)], axis=0).astype(BF16)
    w_up = wup_ref[...].astype(BF16)
    scale = GLA_DK ** -0.5

    def decayed_operands(si, slot):
        rs = slice(si * sub, (si + 1) * sub)
        x = _dot(r_ref[rs, :].astype(BF16), w_up) + b_ref[...]
        la = (jnp.minimum(x, 0.0) - jnp.log(1.0 + jnp.exp(-jnp.abs(x)))) * (1.0 / GLA_TAU)
        sums = _dot(sum_mat, la.astype(BF16))
        cum, tot = sums[:sub], sums[sub:]
        e_tot = jnp.exp(tot)
        et_ref[slot] = e_tot
        qd_ref[slot] = (q_ref[rs, :].astype(F32) * (scale * jnp.exp(cum))).astype(BF16)
        k_in = k_ref[rs, :].astype(F32) * jnp.exp(-cum)
        kst_ref[slot] = (k_in * e_tot).astype(BF16)
        kin_ref[slot] = k_in.astype(BF16)

    def heads(si, slot):
        rs = slice(si * sub, (si + 1) * sub)
        for h in range(GLA_HEADS):
            ks = slice(h * GLA_DK, (h + 1) * GLA_DK)
            vs = slice(h * GLA_DV, (h + 1) * GLA_DV)
            v = v_ref[rs, vs]
            qd = qd_ref[slot, :, ks]
            sc = lax.dot_general(qd, kin_ref[slot, :, ks], NT_DIMS, preferred_element_type=F32)
            o = _dot(jnp.where(tri, sc, 0.0).astype(BF16), v)
            st = st_ref[h]
            inter = [None] * nch
            for ci in (range(nch) if forward else range(nch - 1, -1, -1)):
                cr = slice(ci * c, (ci + 1) * c)
                inter[ci] = lax.dot_general(qd[cr, :], st.astype(BF16), NT_DIMS, preferred_element_type=F32)
                upd = lax.dot_general(v[cr, :], kst_ref[slot, cr, ks], TN_DIMS, preferred_element_type=F32)
                st = st * et_ref[slot, ci * c:ci * c + 1, ks] + upd
            st_ref[h] = st
            o = o + jnp.concatenate(inter, axis=0)
            seq_rows = pl.ds(pl.multiple_of(seq_row0 + si * sub, sub), sub)
            if forward:
                of_ref[seq_rows, vs] = o
            else:
                y = _rms(of_ref[seq_rows, vs] + o, g_ref[...])
                gg = gg_ref[rs, vs].astype(F32)
                o_ref[rs, vs] = (y * (gg * jax.nn.sigmoid(gg))).astype(o_ref.dtype)

    order = list(range(nsub) if forward else range(nsub - 1, -1, -1))
    decayed_operands(order[0], 0)
    for n, si in enumerate(order):
        if n + 1 < nsub:
            decayed_operands(order[n + 1], (n + 1) % 2)
        heads(si, n % 2)


def _gla_kernel(q_ref, k_ref, v_ref, gg_ref, r_ref, wup_ref, b_ref, g_ref,
                o_ref, of_ref, st_ref, *stage_refs, rows, nblk, sub):
    ph = pl.program_id(1)
    i = pl.program_id(2)

    @pl.when(i == 0)
    def _():
        st_ref[...] = jnp.zeros_like(st_ref)

    refs = (q_ref, k_ref, v_ref, gg_ref, r_ref, wup_ref, b_ref, g_ref, o_ref, of_ref, st_ref) + stage_refs

    @pl.when(ph == 0)
    def _():
        _gla_block(1, *refs, rows=rows, sub=sub, seq_row0=i * rows)

    @pl.when(ph == 1)
    def _():
        _gla_block(-1, *refs, rows=rows, sub=sub, seq_row0=(nblk - 1 - i) * rows)


def _gla(layer, proj, rank, wup_pad, b_alpha, g_out, *, batch, seq, rows=512, sub=128):
    nblk = seq // rows
    t = batch * seq

    def row_block(b, ph, i):
        return b * nblk + jnp.where(ph == 0, i, nblk - 1 - i)

    def parked_block(b, ph, i):
        return b * nblk + jnp.where(ph == 0, nblk - 1, nblk - 1 - i)

    kern = functools.partial(_gla_kernel, rows=rows, nblk=nblk, sub=sub)
    return pl.pallas_call(
        kern,
        grid=(batch, 2, nblk),
        in_specs=[
            pl.BlockSpec((rows, GLA_KEY), lambda b, ph, i: (row_block(b, ph, i), OFF_GQ // GLA_KEY)),
            pl.BlockSpec((rows, GLA_KEY), lambda b, ph, i: (row_block(b, ph, i), OFF_GK // GLA_KEY)),
            pl.BlockSpec((rows, GLA_VAL), lambda b, ph, i: (row_block(b, ph, i), OFF_GV // GLA_VAL)),
            pl.BlockSpec((rows, GLA_VAL), lambda b, ph, i: (parked_block(b, ph, i), OFF_GG // GLA_VAL)),
            pl.BlockSpec((rows, RANK_PAD), lambda b, ph, i: (row_block(b, ph, i), 0)),
            pl.BlockSpec((None, None, RANK_PAD, GLA_KEY), lambda b, ph, i: (layer, ph, 0, 0)),
            pl.BlockSpec((None, None, 1, GLA_KEY), lambda b, ph, i: (layer, ph, 0, 0)),
            _layer_resident(layer, (1, GLA_DV)),
        ],
        out_specs=pl.BlockSpec((rows, GLA_VAL), lambda b, ph, i: (parked_block(b, ph, i), 0)),
        out_shape=jax.ShapeDtypeStruct((t, GLA_VAL), BF16),
        scratch_shapes=[
            pltpu.VMEM((seq, GLA_VAL), F32),
            pltpu.VMEM((GLA_HEADS, GLA_DV, GLA_DK), F32),
            pltpu.VMEM((2, sub, GLA_KEY), BF16),
            pltpu.VMEM((2, sub, GLA_KEY), BF16),
            pltpu.VMEM((2, sub, GLA_KEY), BF16),
            pltpu.VMEM((2, sub, GLA_KEY), F32),
        ],
        compiler_params=pltpu.CompilerParams(
            dimension_semantics=("parallel", "arbitrary", "arbitrary"),
            vmem_limit_bytes=VMEM_LIMIT_BYTES,
        ),
        name="gla",
    )(proj, proj, proj, proj, rank, wup_pad, b_alpha, g_out)


def _attn_kernel(q_ref, k_ref, v_ref, cos_ref, sa_ref, sb_ref, gq_ref, gk_ref,
                 o_ref, kn_ref, v1_ref, *s_refs, tq, rt, kc):
    qi = pl.program_id(2)
    seq = kn_ref.shape[0]

    def rope(x, rows):
        return (x * cos_ref[rows, :]
                + pltpu.roll(x, HEAD_DIM - 32, 1) * sa_ref[rows, :]
                + pltpu.roll(x, 32, 1) * sb_ref[rows, :])

    @pl.when(qi == 0)
    def _():
        k = _rms(k_ref[...].astype(F32), gk_ref[...])
        kn_ref[...] = rope(k, slice(None)).astype(BF16)
        v1_ref[:, :HEAD_DIM] = v_ref[...]
        v1_ref[:, HEAD_DIM:] = jnp.ones((seq, HEAD_DIM), BF16)

    scale = HEAD_DIM ** -0.5 * LOG2_E

    def pass1(r, hh, s_ref):
        x = _rms(q_ref[r * rt:(r + 1) * rt, hh * HEAD_DIM:(hh + 1) * HEAD_DIM].astype(F32), gq_ref[...])
        rows = pl.ds(pl.multiple_of(qi * tq + r * rt, rt), rt)
        q = (rope(x, rows) * scale).astype(BF16)
        m = jnp.full((rt, HEAD_DIM), -jnp.inf, F32)
        for c0 in range(0, seq, kc):
            s = lax.dot_general(q, kn_ref[c0:c0 + kc, :], NT_DIMS, preferred_element_type=F32)
            s_ref[:, c0:c0 + kc] = s
            for l0 in range(0, kc, HEAD_DIM):
                m = jnp.maximum(m, s[:, l0:l0 + HEAD_DIM])
        return jnp.max(m, axis=-1, keepdims=True)

    def pass2(r, hh, s_ref, m):
        acc = jnp.zeros((rt, 2 * HEAD_DIM), F32)
        for c0 in range(0, seq, kc):
            p = jnp.exp2((s_ref[:, c0:c0 + kc] - m).astype(BF16))
            acc = acc + _dot(p, v1_ref[c0:c0 + kc, :])
        o = acc[:, :HEAD_DIM] / acc[:, HEAD_DIM:HEAD_DIM + 1]
        o_ref[r * rt:(r + 1) * rt, hh * HEAD_DIM:(hh + 1) * HEAD_DIM] = o.astype(o_ref.dtype)

    pending = None
    for t, (r, hh) in enumerate((r, hh) for r in range(tq // rt) for hh in range(ATTN_GROUP)):
        s_ref = s_refs[t % len(s_refs)]
        m = pass1(r, hh, s_ref)
        if pending is not None:
            pass2(*pending)
        pending = (r, hh, s_ref, m)
    pass2(*pending)


def _attn(layer, proj, cos_t, sa_t, sb_t, g_q, g_k, *, batch, seq, tq=512, rt=128, kc=512):
    nq = seq // tq
    t = batch * seq
    gw = ATTN_GROUP * HEAD_DIM
    kern = functools.partial(_attn_kernel, tq=tq, rt=rt, kc=kc)
    return pl.pallas_call(
        kern,
        grid=(batch, ATTN_KV_HEADS, nq),
        in_specs=[
            pl.BlockSpec((tq, gw), lambda b, g, i: (b * nq + i, OFF_AQ // gw + g)),
            pl.BlockSpec((seq, HEAD_DIM), lambda b, g, i: (b, OFF_AK // HEAD_DIM + g)),
            pl.BlockSpec((seq, HEAD_DIM), lambda b, g, i: (b, OFF_AV // HEAD_DIM + g)),
            _resident((seq, HEAD_DIM)),
            _resident((seq, HEAD_DIM)),
            _resident((seq, HEAD_DIM)),
            _layer_resident(layer, (1, HEAD_DIM)),
            _layer_resident(layer, (1, HEAD_DIM)),
        ],
        out_specs=pl.BlockSpec((tq, gw), lambda b, g, i: (b * nq + i, g)),
        out_shape=jax.ShapeDtypeStruct((t, ATTN_Q), BF16),
        scratch_shapes=[
            pltpu.VMEM((seq, HEAD_DIM), BF16),
            pltpu.VMEM((seq, 2 * HEAD_DIM), BF16),
            pltpu.VMEM((rt, seq), F32),
            pltpu.VMEM((rt, seq), F32),
        ],
        compiler_params=pltpu.CompilerParams(
            dimension_semantics=("parallel", "parallel", "arbitrary"),
            vmem_limit_bytes=VMEM_LIMIT_BYTES,
        ),
        name="attn",
    )(proj, proj, proj, cos_t, sa_t, sb_t, g_q, g_k)


def _merge_kernel(a_ref, at_ref, ga_ref, gb_ref, h_ref, woa_ref, wob_ref, wout_ref, g_ref, o_ref,
                  *, sub):
    for r0 in range(0, o_ref.shape[0], sub):
        rs = slice(r0, r0 + sub)
        branch_a = _dot(a_ref[rs, :], woa_ref[...])
        branch_b = _dot(at_ref[rs, :], wob_ref[...])
        mixed = (jax.nn.sigmoid(ga_ref[rs, :].astype(F32)) * branch_a
                 + jax.nn.sigmoid(gb_ref[rs, :].astype(F32)) * branch_b)
        y = _dot(mixed.astype(BF16), wout_ref[...])
        o_ref[rs, :] = h_ref[rs, :] + _rms(y, g_ref[...])


def _merge(layer, gla_out, attn_out, proj, h, w_o_gla, w_o_attn, w_out, g_post, *, tm=512, sub=256):
    t = h.shape[0]
    row = lambda i: (i, 0)
    return pl.pallas_call(
        functools.partial(_merge_kernel, sub=sub),
        grid=(t // tm,),
        in_specs=[
            pl.BlockSpec((tm, GLA_VAL), row),
            pl.BlockSpec((tm, ATTN_Q), row),
            pl.BlockSpec((tm, D_MODEL), lambda i: (i, OFF_GATE_A // D_MODEL)),
            pl.BlockSpec((tm, D_MODEL), lambda i: (i, OFF_GATE_B // D_MODEL)),
            pl.BlockSpec((tm, D_MODEL), row),
            _layer_resident(layer, (GLA_VAL, D_MODEL)),
            _layer_resident(layer, (ATTN_Q, D_MODEL)),
            _layer_resident(layer, (D_MODEL, D_MODEL)),
            _layer_resident(layer, (1, D_MODEL)),
        ],
        out_specs=pl.BlockSpec((tm, D_MODEL), row),
        out_shape=jax.ShapeDtypeStruct((t, D_MODEL), F32),
        compiler_params=pltpu.CompilerParams(
            dimension_semantics=("parallel",),
            vmem_limit_bytes=VMEM_LIMIT_BYTES,
        ),
        name="merge",
    )(gla_out, attn_out, proj, proj, h, w_o_gla, w_o_attn, w_out, g_post)


def _ffn_kernel(h_ref, p_ref, gpre_ref, win_ref, wout_ref, gpost_ref, wpp_ref, wpg_ref, gple_ref,
                o_ref, *, hidden_chunk, sub):
    for r0 in range(0, o_ref.shape[0], sub):
        rs = slice(r0, r0 + sub)
        h = h_ref[rs, :]
        x = _rms(h, gpre_ref[...]).astype(BF16)
        acc = jnp.zeros(h.shape, F32)
        for c0 in range(0, FFN_HIDDEN, hidden_chunk):
            gate = _dot(x, win_ref[:, c0:c0 + hidden_chunk])
            up = _dot(x, win_ref[:, FFN_HIDDEN + c0:FFN_HIDDEN + c0 + hidden_chunk])
            act = (gate * jax.nn.sigmoid(gate)) * up
            acc = acc + _dot(act.astype(BF16), wout_ref[c0:c0 + hidden_chunk, :])
        h = h + _rms(acc, gpost_ref[...])
        e = _dot(p_ref[rs, :].astype(BF16), wpp_ref[...])
        gate = jax.nn.sigmoid(_dot(h.astype(BF16), wpg_ref[...]))
        o_ref[rs, :] = h + _rms(gate * e, gple_ref[...])


def _ffn(layer, h, p, g_pre, w_in, w_out, g_post, w_pp, w_pg, g_ple, *, tm=512, sub=256,
         hidden_chunk=FFN_HIDDEN):
    t = h.shape[0]
    row = lambda i: (i, 0)
    kern = functools.partial(_ffn_kernel, hidden_chunk=hidden_chunk, sub=sub)
    return pl.pallas_call(
        kern,
        grid=(t // tm,),
        in_specs=[
            pl.BlockSpec((tm, D_MODEL), row),
            pl.BlockSpec((None, tm, PLE_DIM), lambda i: (layer, i, 0)),
            _layer_resident(layer, (1, D_MODEL)),
            _layer_resident(layer, (D_MODEL, 2 * FFN_HIDDEN)),
            _layer_resident(layer, (FFN_HIDDEN, D_MODEL)),
            _layer_resident(layer, (1, D_MODEL)),
            _layer_resident(layer, (PLE_DIM, D_MODEL)),
            _layer_resident(layer, (D_MODEL, D_MODEL)),
            _layer_resident(layer, (1, D_MODEL)),
        ],
        out_specs=pl.BlockSpec((tm, D_MODEL), row),
        out_shape=jax.ShapeDtypeStruct((t, D_MODEL), F32),
        compiler_params=pltpu.CompilerParams(
            dimension_semantics=("parallel",),
            vmem_limit_bytes=VMEM_LIMIT_BYTES,
        ),
        name="ffn",
    )(h, p, g_pre, w_in, w_out, g_post, w_pp, w_pg, g_ple)


def _rope_tables(seq):
    rows = seq // GRID_W
    row = jnp.repeat(jnp.arange(rows, dtype=F32), GRID_W)
    col = jnp.tile(jnp.arange(GRID_W, dtype=F32), rows)
    inv = ROPE_THETA ** (-jnp.arange(0, ROPE_AXIS_DIM, 2, dtype=F32) / ROPE_AXIS_DIM)
    ang = jnp.stack([row[:, None] * inv, col[:, None] * inv], axis=1)
    cos, sin = jnp.cos(ang), jnp.sin(ang)
    zero = jnp.zeros_like(sin)
    lanes = lambda a, b: jnp.stack([a, b], axis=2).reshape(seq, HEAD_DIM)
    return lanes(cos, cos), lanes(-sin, zero), lanes(zero, sin)


def _permute_w_in(w_in):
    sizes = (GLA_KEY, GLA_KEY, GLA_VAL, GLA_VAL, GLA_RANK, GLA_RANK,
             ATTN_Q, ATTN_KV, ATTN_KV, D_MODEL, D_MODEL)
    offs = np.cumsum((0,) + sizes)
    gq, gk, gv, gg, ra_f, ra_b, aq, ak, av, gate_a, gate_b = (
        w_in[:, :, offs[n]:offs[n + 1]] for n in range(len(sizes)))
    main = jnp.concatenate([gv, gg, aq, gate_a, gate_b, gq, gk, ak, av], axis=2)
    rank = jnp.concatenate([ra_f, ra_b], axis=2)
    rank = jnp.pad(rank, ((0, 0), (0, 0), (0, RANK_PAD - 2 * GLA_RANK)))
    return main.astype(BF16), rank.astype(BF16)


def _pad_w_alpha(w_alpha_up):
    pads = [((0, 0), (d * GLA_RANK, RANK_PAD - (d + 1) * GLA_RANK), (0, 0)) for d in range(2)]
    return jnp.stack([jnp.pad(w_alpha_up[:, d], pads[d]) for d in range(2)], axis=1)


def kernel(x, p, g_mix_pre, w_in, w_alpha_up, b_alpha, g_gla_out, g_q_norm, g_k_norm, w_o_gla,
           w_o_attn, w_out, g_mix_post, g_ffn_pre, w_ffn_in, w_ffn_out, g_ffn_post, w_ple_proj,
           w_ple_gate, g_ple_post):
    batch, seq, _ = x.shape
    depth = p.shape[0]
    t = batch * seq
    cos_t, sa_t, sb_t = _rope_tables(seq)
    gain = lambda g: g.reshape(depth, 1, -1)
    w_main, w_rank = _permute_w_in(w_in)
    wup_pad = _pad_w_alpha(w_alpha_up)
    b_alpha = b_alpha.reshape(depth, 2, 1, GLA_KEY)
    w_o_gla, w_o_attn, w_out, w_ffn_in, w_ffn_out, w_ple_proj, w_ple_gate = (
        w.astype(BF16) for w in (w_o_gla, w_o_attn, w_out, w_ffn_in, w_ffn_out, w_ple_proj, w_ple_gate))
    p = p.reshape(depth, t, PLE_DIM)
    h = x.reshape(t, D_MODEL)
    for i in range(depth):
        proj, rank = _proj(i, h, gain(g_mix_pre), w_main, w_rank)
        gla_out = _gla(i, proj, rank, wup_pad, b_alpha, gain(g_gla_out), batch=batch, seq=seq)
        attn_out = _attn(i, proj, cos_t, sa_t, sb_t, gain(g_q_norm), gain(g_k_norm),
                         batch=batch, seq=seq)
        h = _merge(i, gla_out, attn_out, proj, h, w_o_gla, w_o_attn, w_out, gain(g_mix_post))
        h = _ffn(i, h, p, gain(g_ffn_pre), w_ffn_in, w_ffn_out, gain(g_ffn_post), w_ple_proj,
                 w_ple_gate, gain(g_ple_post))
    return h.reshape(batch, seq, D_MODEL)
```

```python
import functools

import jax
import jax.numpy as jnp
import numpy as np
from jax import lax
from jax.experimental import pallas as pl
from jax.experimental.pallas import tpu as pltpu

F32 = jnp.float32
BF16 = jnp.bfloat16

D_MODEL = 1024
GRID_W = 64
PLE_DIM = 256
EPS = 1e-6
GLA_HEADS = 4
GLA_DK = 128
GLA_DV = 256
GLA_KEY = GLA_HEADS * GLA_DK
GLA_VAL = GLA_HEADS * GLA_DV
GLA_RANK = 16
GLA_TAU = 16.0
GLA_CHUNK = 64
GLA_CHUNK_SHIFT = 6
assert 1 << GLA_CHUNK_SHIFT == GLA_CHUNK
ATTN_HEADS = 8
ATTN_KV_HEADS = 2
ATTN_GROUP = ATTN_HEADS // ATTN_KV_HEADS
HEAD_DIM = 128
ATTN_Q = ATTN_HEADS * HEAD_DIM
ATTN_KV = ATTN_KV_HEADS * HEAD_DIM
ROPE_THETA = 10000.0
ROPE_AXIS_DIM = HEAD_DIM // 2
FFN_HIDDEN = 2816
LOG2_E = 1.4426950408889634

OFF_GV = 0
OFF_GG = 1024
OFF_AQ = 2048
OFF_GATE_A = 3072
OFF_GATE_B = 4096
OFF_GQ = 5120
OFF_GK = 5632
OFF_AK = 6144
OFF_AV = 6400
PROJ_WIDTH = 6656
RANK_PAD = 128

VMEM_LIMIT_BYTES = 56 * 1024 * 1024

NT_DIMS = (((1,), (1,)), ((), ()))
TN_DIMS = (((0,), (0,)), ((), ()))


def _rms(x, g):
    return x * lax.rsqrt(jnp.mean(x * x, axis=-1, keepdims=True) + EPS) * g


def _dot(a, b):
    return jnp.dot(a, b, preferred_element_type=F32)


def _resident(shape):
    zeros = (0,) * len(shape)
    return pl.BlockSpec(shape, lambda *_: zeros, pipeline_mode=pl.Buffered(1))


def _layer_resident(layer, shape):
    index = (layer,) + (0,) * len(shape)
    return pl.BlockSpec((None,) + shape, lambda *_: index, pipeline_mode=pl.Buffered(1))


def _proj_kernel(x_ref, g_ref, w_ref, wr_ref, o_ref, r_ref, *, tn):
    u = _rms(x_ref[...], g_ref[...]).astype(BF16)
    r_ref[...] = _dot(u, wr_ref[...])
    for n0 in range(0, w_ref.shape[1], tn):
        o_ref[:, n0:n0 + tn] = _dot(u, w_ref[:, n0:n0 + tn]).astype(o_ref.dtype)


def _proj(layer, h, g, w, w_rank, *, tm=512, tn=512):
    t = h.shape[0]
    n = w.shape[2]
    row = lambda i: (i, 0)
    return pl.pallas_call(
        functools.partial(_proj_kernel, tn=tn),
        grid=(t // tm,),
        in_specs=[
            pl.BlockSpec((tm, D_MODEL), row),
            _layer_resident(layer, (1, D_MODEL)),
            _layer_resident(layer, (D_MODEL, n)),
            _layer_resident(layer, (D_MODEL, RANK_PAD)),
        ],
        out_specs=[
            pl.BlockSpec((tm, n), row),
            pl.BlockSpec((tm, RANK_PAD), row),
        ],
        out_shape=[
            jax.ShapeDtypeStruct((t, n), BF16),
            jax.ShapeDtypeStruct((t, RANK_PAD), F32),
        ],
        compiler_params=pltpu.CompilerParams(
            dimension_semantics=("parallel",),
            vmem_limit_bytes=VMEM_LIMIT_BYTES,
        ),
        name="proj",
    )(h, g, w, w_rank)


def _gla_block(direction, q_ref, k_ref, v_ref, gg_ref, r_ref, wup_ref, b_ref, g_ref,
               o_ref, of_ref, st_ref, qd_ref, kin_ref, kst_ref, et_ref, *, rows, sub, seq_row0):
    c = GLA_CHUNK
    nsub = rows // sub
    nch = sub // c
    forward = direction > 0
    row = lax.broadcasted_iota(jnp.int32, (sub, sub), 0)
    col = lax.broadcasted_iota(jnp.int32, (sub, sub), 1)
    same = (row >> GLA_CHUNK_SHIFT) == (col >> GLA_CHUNK_SHIFT)
    tri = jnp.logical_and(same, (col <= row) if forward else (col >= row))
    sum_mat = jnp.concatenate([jnp.where(tri, 1.0, 0.0), jnp.where(same, 1.0, 0.0)], axis=0).astype(BF16)
    w_up = wup_ref[...].astype(BF16)
    scale = GLA_DK ** -0.5

    def decayed_operands(si, slot):
        rs = slice(si * sub, (si + 1) * sub)
        x = _dot(r_ref[rs, :].astype(BF16), w_up) + b_ref[...]
        la = (jnp.minimum(x, 0.0) - jnp.log(1.0 + jnp.exp(-jnp.abs(x)))) * (1.0 / GLA_TAU)
        sums = _dot(sum_mat, la.astype(BF16))
        cum, tot = sums[:sub], sums[sub:]
        e_tot = jnp.exp(tot)
        et_ref[slot] = e_tot
        qd_ref[slot] = (q_ref[rs, :].astype(F32) * (scale * jnp.exp(cum))).astype(BF16)
        k_in = k_ref[rs, :].astype(F32) * jnp.exp(-cum)
        kst_ref[slot] = (k_in * e_tot).astype(BF16)
        kin_ref[slot] = k_in.astype(BF16)

    def heads(si, slot):
        rs = slice(si * sub, (si + 1) * sub)
        for h in range(GLA_HEADS):
            ks = slice(h * GLA_DK, (h + 1) * GLA_DK)
            vs = slice(h * GLA_DV, (h + 1) * GLA_DV)
            v = v_ref[rs, vs]
            qd = qd_ref[slot, :, ks]
            sc = lax.dot_general(qd, kin_ref[slot, :, ks], NT_DIMS, preferred_element_type=F32)
            o = _dot(jnp.where(tri, sc, 0.0).astype(BF16), v)
            st = st_ref[h]
            inter = [None] * nch
            for ci in (range(nch) if forward else range(nch - 1, -1, -1)):
                cr = slice(ci * c, (ci + 1) * c)
                inter[ci] = lax.dot_general(qd[cr, :], st.astype(BF16), NT_DIMS, preferred_element_type=F32)
                upd = lax.dot_general(v[cr, :], kst_ref[slot, cr, ks], TN_DIMS, preferred_element_type=F32)
                st = st * et_ref[slot, ci * c:ci * c + 1, ks] + upd
            st_ref[h] = st
            o = o + jnp.concatenate(inter, axis=0)
            seq_rows = pl.ds(pl.multiple_of(seq_row0 + si * sub, sub), sub)
            if forward:
                of_ref[seq_rows, vs] = o
            else:
                y = _rms(of_ref[seq_rows, vs] + o, g_ref[...])
                gg = gg_ref[rs, vs].astype(F32)
                o_ref[rs, vs] = (y * (gg * jax.nn.sigmoid(gg))).astype(o_ref.dtype)

    order = list(range(nsub) if forward else range(nsub - 1, -1, -1))
    decayed_operands(order[0], 0)
    for n, si in enumerate(order):
        if n + 1 < nsub:
            decayed_operands(order[n + 1], (n + 1) % 2)
        heads(si, n % 2)


def _gla_kernel(q_ref, k_ref, v_ref, gg_ref, r_ref, wup_ref, b_ref, g_ref,
                o_ref, of_ref, st_ref, *stage_refs, rows, nblk, sub):
    ph = pl.program_id(1)
    i = pl.program_id(2)

    @pl.when(i == 0)
    def _():
        st_ref[...] = jnp.zeros_like(st_ref)

    refs = (q_ref, k_ref, v_ref, gg_ref, r_ref, wup_ref, b_ref, g_ref, o_ref, of_ref, st_ref) + stage_refs

    @pl.when(ph == 0)
    def _():
        _gla_block(1, *refs, rows=rows, sub=sub, seq_row0=i * rows)

    @pl.when(ph == 1)
    def _():
        _gla_block(-1, *refs, rows=rows, sub=sub, seq_row0=(nblk - 1 - i) * rows)


def _gla(layer, proj, rank, wup_pad, b_alpha, g_out, *, batch, seq, rows=512, sub=128):
    nblk = seq // rows
    t = batch * seq

    def row_block(b, ph, i):
        return b * nblk + jnp.where(ph == 0, i, nblk - 1 - i)

    def parked_block(b, ph, i):
        return b * nblk + jnp.where(ph == 0, nblk - 1, nblk - 1 - i)

    kern = functools.partial(_gla_kernel, rows=rows, nblk=nblk, sub=sub)
    return pl.pallas_call(
        kern,
        grid=(batch, 2, nblk),
        in_specs=[
            pl.BlockSpec((rows, GLA_KEY), lambda b, ph, i: (row_block(b, ph, i), OFF_GQ // GLA_KEY)),
            pl.BlockSpec((rows, GLA_KEY), lambda b, ph, i: (row_block(b, ph, i), OFF_GK // GLA_KEY)),
            pl.BlockSpec((rows, GLA_VAL), lambda b, ph, i: (row_block(b, ph, i), OFF_GV // GLA_VAL)),
            pl.BlockSpec((rows, GLA_VAL), lambda b, ph, i: (parked_block(b, ph, i), OFF_GG // GLA_VAL)),
            pl.BlockSpec((rows, RANK_PAD), lambda b, ph, i: (row_block(b, ph, i), 0)),
            pl.BlockSpec((None, None, RANK_PAD, GLA_KEY), lambda b, ph, i: (layer, ph, 0, 0)),
            pl.BlockSpec((None, None, 1, GLA_KEY), lambda b, ph, i: (layer, ph, 0, 0)),
            _layer_resident(layer, (1, GLA_DV)),
        ],
        out_specs=pl.BlockSpec((rows, GLA_VAL), lambda b, ph, i: (parked_block(b, ph, i), 0)),
        out_shape=jax.ShapeDtypeStruct((t, GLA_VAL), BF16),
        scratch_shapes=[
            pltpu.VMEM((seq, GLA_VAL), F32),
            pltpu.VMEM((GLA_HEADS, GLA_DV, GLA_DK), F32),
            pltpu.VMEM((2, sub, GLA_KEY), BF16),
            pltpu.VMEM((2, sub, GLA_KEY), BF16),
            pltpu.VMEM((2, sub, GLA_KEY), BF16),
            pltpu.VMEM((2, sub, GLA_KEY), F32),
        ],
        compiler_params=pltpu.CompilerParams(
            dimension_semantics=("parallel", "arbitrary", "arbitrary"),
            vmem_limit_bytes=VMEM_LIMIT_BYTES,
        ),
        name="gla",
    )(proj, proj, proj, proj, rank, wup_pad, b_alpha, g_out)


def _attn_kernel(q_ref, k_ref, v_ref, cos_ref, sa_ref, sb_ref, gq_ref, gk_ref,
                 o_ref, kn_ref, v1_ref, m_ref, *s_refs, tq, rt, kc):
    qi = pl.program_id(2)
    nq = pl.num_programs(2)
    seq = kn_ref.shape[0]
    tiles = [(r, hh) for r in range(tq // rt) for hh in range(ATTN_GROUP)]
    assert len(s_refs) == 2 and len(tiles) % 2 == 0
    carried_s_ref = s_refs[(len(tiles) - 1) % 2]

    def rope(x, rows):
        return (x * cos_ref[rows, :]
                + pltpu.roll(x, HEAD_DIM - 32, 1) * sa_ref[rows, :]
                + pltpu.roll(x, 32, 1) * sb_ref[rows, :])

    @pl.when(qi == 0)
    def _():
        k = _rms(k_ref[...].astype(F32), gk_ref[...])
        kn_ref[...] = rope(k, slice(None)).astype(BF16)
        v1_ref[:, :HEAD_DIM] = v_ref[...]
        v1_ref[:, HEAD_DIM:] = jnp.ones((seq, HEAD_DIM), BF16)
        carried_s_ref[...] = jnp.zeros_like(carried_s_ref)
        m_ref[...] = jnp.zeros_like(m_ref)

    scale = HEAD_DIM ** -0.5 * LOG2_E

    def pass1(r, hh, s_ref):
        x = _rms(q_ref[r * rt:(r + 1) * rt, hh * HEAD_DIM:(hh + 1) * HEAD_DIM].astype(F32), gq_ref[...])
        rows = pl.ds(pl.multiple_of(qi * tq + r * rt, rt), rt)
        q = (rope(x, rows) * scale).astype(BF16)
        m = jnp.full((rt, HEAD_DIM), -jnp.inf, F32)
        for c0 in range(0, seq, kc):
            s = lax.dot_general(q, kn_ref[c0:c0 + kc, :], NT_DIMS, preferred_element_type=F32)
            s_ref[:, c0:c0 + kc] = s
            for l0 in range(0, kc, HEAD_DIM):
                m = jnp.maximum(m, s[:, l0:l0 + HEAD_DIM])
        return jnp.max(m, axis=-1, keepdims=True)

    def pass2(row0, hh, s_ref, m):
        acc = jnp.zeros((rt, 2 * HEAD_DIM), F32)
        for c0 in range(0, seq, kc):
            p = jnp.exp2((s_ref[:, c0:c0 + kc] - m).astype(BF16))
            acc = acc + _dot(p, v1_ref[c0:c0 + kc, :])
        o = acc[:, :HEAD_DIM] / acc[:, HEAD_DIM:HEAD_DIM + 1]
        rows = pl.ds(pl.multiple_of(row0, rt), rt)
        o_ref[rows, hh * HEAD_DIM:(hh + 1) * HEAD_DIM] = o.astype(o_ref.dtype)

    last_r, last_hh = tiles[-1]
    prev_block = jnp.where(qi == 0, nq - 1, qi - 1)
    pending = (prev_block * tq + last_r * rt, last_hh, carried_s_ref, m_ref[:, :1])
    for t, (r, hh) in enumerate(tiles):
        s_ref = s_refs[t % 2]
        m = pass1(r, hh, s_ref)
        pass2(*pending)
        pending = (qi * tq + r * rt, hh, s_ref, m)
    m_ref[...] = jnp.broadcast_to(pending[3], m_ref.shape)

    @pl.when(qi == nq - 1)
    def _():
        pass2(*pending)


def _attn(layer, proj, cos_t, sa_t, sb_t, g_q, g_k, *, batch, seq, tq=512, rt=128, kc=512):
    nq = seq // tq
    t = batch * seq
    gw = ATTN_GROUP * HEAD_DIM
    kern = functools.partial(_attn_kernel, tq=tq, rt=rt, kc=kc)
    return pl.pallas_call(
        kern,
        grid=(batch, ATTN_KV_HEADS, nq),
        in_specs=[
            pl.BlockSpec((tq, gw), lambda b, g, i: (b * nq + i, OFF_AQ // gw + g)),
            pl.BlockSpec((seq, HEAD_DIM), lambda b, g, i: (b, OFF_AK // HEAD_DIM + g)),
            pl.BlockSpec((seq, HEAD_DIM), lambda b, g, i: (b, OFF_AV // HEAD_DIM + g)),
            _resident((seq, HEAD_DIM)),
            _resident((seq, HEAD_DIM)),
            _resident((seq, HEAD_DIM)),
            _layer_resident(layer, (1, HEAD_DIM)),
            _layer_resident(layer, (1, HEAD_DIM)),
        ],
        out_specs=pl.BlockSpec((seq, gw), lambda b, g, i: (b, g)),
        out_shape=jax.ShapeDtypeStruct((t, ATTN_Q), BF16),
        scratch_shapes=[
            pltpu.VMEM((seq, HEAD_DIM), BF16),
            pltpu.VMEM((seq, 2 * HEAD_DIM), BF16),
            pltpu.VMEM((rt, HEAD_DIM), F32),
            pltpu.VMEM((rt, seq), F32),
            pltpu.VMEM((rt, seq), F32),
        ],
        compiler_params=pltpu.CompilerParams(
            dimension_semantics=("parallel", "parallel", "arbitrary"),
            vmem_limit_bytes=VMEM_LIMIT_BYTES,
        ),
        name="attn",
    )(proj, proj, proj, cos_t, sa_t, sb_t, g_q, g_k)


def _merge_kernel(a_ref, at_ref, ga_ref, gb_ref, h_ref, woa_ref, wob_ref, wout_ref, g_ref, o_ref,
                  *, sub):
    for r0 in range(0, o_ref.shape[0], sub):
        rs = slice(r0, r0 + sub)
        branch_a = _dot(a_ref[rs, :], woa_ref[...])
        branch_b = _dot(at_ref[rs, :], wob_ref[...])
        mixed = (jax.nn.sigmoid(ga_ref[rs, :].astype(F32)) * branch_a
                 + jax.nn.sigmoid(gb_ref[rs, :].astype(F32)) * branch_b)
        y = _dot(mixed.astype(BF16), wout_ref[...])
        o_ref[rs, :] = h_ref[rs, :] + _rms(y, g_ref[...])


def _merge(layer, gla_out, attn_out, proj, h, w_o_gla, w_o_attn, w_out, g_post, *, tm=1024, sub=256):
    t = h.shape[0]
    row = lambda i: (i, 0)
    return pl.pallas_call(
        functools.partial(_merge_kernel, sub=sub),
        grid=(t // tm,),
        in_specs=[
            pl.BlockSpec((tm, GLA_VAL), row),
            pl.BlockSpec((tm, ATTN_Q), row),
            pl.BlockSpec((tm, D_MODEL), lambda i: (i, OFF_GATE_A // D_MODEL)),
            pl.BlockSpec((tm, D_MODEL), lambda i: (i, OFF_GATE_B // D_MODEL)),
            pl.BlockSpec((tm, D_MODEL), row),
            _layer_resident(layer, (GLA_VAL, D_MODEL)),
            _layer_resident(layer, (ATTN_Q, D_MODEL)),
            _layer_resident(layer, (D_MODEL, D_MODEL)),
            _layer_resident(layer, (1, D_MODEL)),
        ],
        out_specs=pl.BlockSpec((tm, D_MODEL), row),
        out_shape=jax.ShapeDtypeStruct((t, D_MODEL), F32),
        compiler_params=pltpu.CompilerParams(
            dimension_semantics=("parallel",),
            vmem_limit_bytes=VMEM_LIMIT_BYTES,
        ),
        name="merge",
    )(gla_out, attn_out, proj, proj, h, w_o_gla, w_o_attn, w_out, g_post)


def _ffn_kernel(h_ref, p_ref, gpre_ref, win_ref, wout_ref, gpost_ref, wpp_ref, wpg_ref, gple_ref,
                o_ref, *, hidden_chunk, sub):
    for r0 in range(0, o_ref.shape[0], sub):
        rs = slice(r0, r0 + sub)
        h = h_ref[rs, :]
        x = _rms(h, gpre_ref[...]).astype(BF16)
        acc = jnp.zeros(h.shape, F32)
        for c0 in range(0, FFN_HIDDEN, hidden_chunk):
            gate = _dot(x, win_ref[:, c0:c0 + hidden_chunk])
            up = _dot(x, win_ref[:, FFN_HIDDEN + c0:FFN_HIDDEN + c0 + hidden_chunk])
            act = (gate * jax.nn.sigmoid(gate)) * up
            acc = acc + _dot(act.astype(BF16), wout_ref[c0:c0 + hidden_chunk, :])
        h = h + _rms(acc, gpost_ref[...])
        e = _dot(p_ref[rs, :].astype(BF16), wpp_ref[...])
        gate = jax.nn.sigmoid(_dot(h.astype(BF16), wpg_ref[...]))
        o_ref[rs, :] = h + _rms(gate * e, gple_ref[...])


def _ffn(layer, h, p, g_pre, w_in, w_out, g_post, w_pp, w_pg, g_ple, *, seq, tm=1024, sub=256,
         hidden_chunk=FFN_HIDDEN):
    t = h.shape[0]
    row = lambda i: (i, 0)
    tiles_per_seq = seq // tm
    kern = functools.partial(_ffn_kernel, hidden_chunk=hidden_chunk, sub=sub)
    return pl.pallas_call(
        kern,
        grid=(t // tm,),
        in_specs=[
            pl.BlockSpec((tm, D_MODEL), row),
            pl.BlockSpec((None, None, tm, PLE_DIM),
                         lambda i: (layer, i // tiles_per_seq, i % tiles_per_seq, 0)),
            _layer_resident(layer, (1, D_MODEL)),
            _layer_resident(layer, (D_MODEL, 2 * FFN_HIDDEN)),
            _layer_resident(layer, (FFN_HIDDEN, D_MODEL)),
            _layer_resident(layer, (1, D_MODEL)),
            _layer_resident(layer, (PLE_DIM, D_MODEL)),
            _layer_resident(layer, (D_MODEL, D_MODEL)),
            _layer_resident(layer, (1, D_MODEL)),
        ],
        out_specs=pl.BlockSpec((tm, D_MODEL), row),
        out_shape=jax.ShapeDtypeStruct((t, D_MODEL), F32),
        compiler_params=pltpu.CompilerParams(
            dimension_semantics=("parallel",),
            vmem_limit_bytes=VMEM_LIMIT_BYTES,
        ),
        name="ffn",
    )(h, p, g_pre, w_in, w_out, g_post, w_pp, w_pg, g_ple)


def _rope_tables(seq):
    rows = seq // GRID_W
    row = jnp.repeat(jnp.arange(rows, dtype=F32), GRID_W)
    col = jnp.tile(jnp.arange(GRID_W, dtype=F32), rows)
    inv = ROPE_THETA ** (-jnp.arange(0, ROPE_AXIS_DIM, 2, dtype=F32) / ROPE_AXIS_DIM)
    ang = jnp.stack([row[:, None] * inv, col[:, None] * inv], axis=1)
    cos, sin = jnp.cos(ang), jnp.sin(ang)
    zero = jnp.zeros_like(sin)
    lanes = lambda a, b: jnp.stack([a, b], axis=2).reshape(seq, HEAD_DIM)
    return lanes(cos, cos), lanes(-sin, zero), lanes(zero, sin)


def _permute_w_in(w_in):
    sizes = (GLA_KEY, GLA_KEY, GLA_VAL, GLA_VAL, GLA_RANK, GLA_RANK,
             ATTN_Q, ATTN_KV, ATTN_KV, D_MODEL, D_MODEL)
    offs = np.cumsum((0,) + sizes)
    gq, gk, gv, gg, ra_f, ra_b, aq, ak, av, gate_a, gate_b = (
        w_in[:, :, offs[n]:offs[n + 1]] for n in range(len(sizes)))
    main = jnp.concatenate([gv, gg, aq, gate_a, gate_b, gq, gk, ak, av], axis=2)
    rank = jnp.concatenate([ra_f, ra_b], axis=2)
    rank = jnp.pad(rank, ((0, 0), (0, 0), (0, RANK_PAD - 2 * GLA_RANK)))
    return main.astype(BF16), rank.astype(BF16)


def _pad_w_alpha(w_alpha_up):
    pads = [((0, 0), (d * GLA_RANK, RANK_PAD - (d + 1) * GLA_RANK), (0, 0)) for d in range(2)]
    return jnp.stack([jnp.pad(w_alpha_up[:, d], pads[d]) for d in range(2)], axis=1)


def kernel(x, p, g_mix_pre, w_in, w_alpha_up, b_alpha, g_gla_out, g_q_norm, g_k_norm, w_o_gla,
           w_o_attn, w_out, g_mix_post, g_ffn_pre, w_ffn_in, w_ffn_out, g_ffn_post, w_ple_proj,
           w_ple_gate, g_ple_post):
    batch, seq, _ = x.shape
    depth = p.shape[0]
    t = batch * seq
    cos_t, sa_t, sb_t = _rope_tables(seq)
    gain = lambda g: g.reshape(depth, 1, -1)
    w_main, w_rank = _permute_w_in(w_in)
    wup_pad = _pad_w_alpha(w_alpha_up)
    b_alpha = b_alpha.reshape(depth, 2, 1, GLA_KEY)
    w_o_gla, w_o_attn, w_out, w_ffn_in, w_ffn_out, w_ple_proj, w_ple_gate = (
        w.astype(BF16) for w in (w_o_gla, w_o_attn, w_out, w_ffn_in, w_ffn_out, w_ple_proj, w_ple_gate))
    h = x.reshape(t, D_MODEL)
    for i in range(depth):
        proj, rank = _proj(i, h, gain(g_mix_pre), w_main, w_rank)
        gla_out = _gla(i, proj, rank, wup_pad, b_alpha, gain(g_gla_out), batch=batch, seq=seq)
        attn_out = _attn(i, proj, cos_t, sa_t, sb_t, gain(g_q_norm), gain(g_k_norm),
                         batch=batch, seq=seq)
        h = _merge(i, gla_out, attn_out, proj, h, w_o_gla, w_o_attn, w_out, gain(g_mix_post))
        h = _ffn(i, h, p, gain(g_ffn_pre), w_ffn_in, w_ffn_out, gain(g_ffn_post), w_ple_proj,
                 w_ple_gate, gain(g_ple_post), seq=seq)
    return h.reshape(batch, seq, D_MODEL)
```

```python
import functools

import jax
import jax.numpy as jnp
import numpy as np
from jax import lax
from jax.experimental import pallas as pl
from jax.experimental.pallas import tpu as pltpu

F32 = jnp.float32
BF16 = jnp.bfloat16

D_MODEL = 1024
GRID_W = 64
PLE_DIM = 256
EPS = 1e-6
GLA_HEADS = 4
GLA_DK = 128
GLA_DV = 256
GLA_KEY = GLA_HEADS * GLA_DK
GLA_VAL = GLA_HEADS * GLA_DV
GLA_RANK = 16
GLA_TAU = 16.0
GLA_CHUNK = 64
GLA_CHUNK_SHIFT = 6
assert 1 << GLA_CHUNK_SHIFT == GLA_CHUNK
ATTN_HEADS = 8
ATTN_KV_HEADS = 2
ATTN_GROUP = ATTN_HEADS // ATTN_KV_HEADS
HEAD_DIM = 128
ATTN_Q = ATTN_HEADS * HEAD_DIM
ATTN_KV = ATTN_KV_HEADS * HEAD_DIM
ROPE_THETA = 10000.0
ROPE_AXIS_DIM = HEAD_DIM // 2
FFN_HIDDEN = 2816
LOG2_E = 1.4426950408889634

OFF_GV = 0
OFF_GG = 1024
OFF_AQ = 2048
OFF_GATE_A = 3072
OFF_GATE_B = 4096
OFF_GQ = 5120
OFF_GK = 5632
OFF_AK = 6144
OFF_AV = 6400
PROJ_WIDTH = 6656
RANK_PAD = 128

VMEM_LIMIT_BYTES = 56 * 1024 * 1024

NT_DIMS = (((1,), (1,)), ((), ()))
TN_DIMS = (((0,), (0,)), ((), ()))


def _rms(x, g):
    return x * lax.rsqrt(jnp.mean(x * x, axis=-1, keepdims=True) + EPS) * g


def _dot(a, b):
    return jnp.dot(a, b, preferred_element_type=F32)


def _resident(shape):
    zeros = (0,) * len(shape)
    return pl.BlockSpec(shape, lambda *_: zeros, pipeline_mode=pl.Buffered(1))


def _layer_resident(layer, shape):
    index = (layer,) + (0,) * len(shape)
    return pl.BlockSpec((None,) + shape, lambda *_: index, pipeline_mode=pl.Buffered(1))


def _proj_kernel(x_ref, g_ref, w_ref, wr_ref, o_ref, r_ref, *, tn):
    u = _rms(x_ref[...], g_ref[...]).astype(BF16)
    r_ref[...] = _dot(u, wr_ref[...])
    for n0 in range(0, w_ref.shape[1], tn):
        o_ref[:, n0:n0 + tn] = _dot(u, w_ref[:, n0:n0 + tn]).astype(o_ref.dtype)


def _proj(layer, h, g, w, w_rank, *, tm=512, tn=512):
    t = h.shape[0]
    n = w.shape[2]
    row = lambda i: (i, 0)
    return pl.pallas_call(
        functools.partial(_proj_kernel, tn=tn),
        grid=(t // tm,),
        in_specs=[
            pl.BlockSpec((tm, D_MODEL), row),
            _layer_resident(layer, (1, D_MODEL)),
            _layer_resident(layer, (D_MODEL, n)),
            _layer_resident(layer, (D_MODEL, RANK_PAD)),
        ],
        out_specs=[
            pl.BlockSpec((tm, n), row),
            pl.BlockSpec((tm, RANK_PAD), row),
        ],
        out_shape=[
            jax.ShapeDtypeStruct((t, n), BF16),
            jax.ShapeDtypeStruct((t, RANK_PAD), F32),
        ],
        compiler_params=pltpu.CompilerParams(
            dimension_semantics=("parallel",),
            vmem_limit_bytes=VMEM_LIMIT_BYTES,
        ),
        name="proj",
    )(h, g, w, w_rank)


def _gla_block(direction, q_ref, k_ref, v_ref, gg_ref, r_ref, wup_ref, b_ref, g_ref,
               o_ref, of_ref, st_ref, qd_ref, kin_ref, kst_ref, et_ref, *, rows, sub, seq_row0):
    c = GLA_CHUNK
    nsub = rows // sub
    nch = sub // c
    forward = direction > 0
    row = lax.broadcasted_iota(jnp.int32, (sub, sub), 0)
    col = lax.broadcasted_iota(jnp.int32, (sub, sub), 1)
    same = (row >> GLA_CHUNK_SHIFT) == (col >> GLA_CHUNK_SHIFT)
    tri = jnp.logical_and(same, (col <= row) if forward else (col >= row))
    sum_mat = jnp.concatenate([jnp.where(tri, 1.0, 0.0), jnp.where(same, 1.0, 0.0)], axis=0).astype(BF16)
    w_up = wup_ref[...].astype(BF16)
    scale = GLA_DK ** -0.5

    def decayed_operands(si, slot):
        rs = slice(si * sub, (si + 1) * sub)
        x = _dot(r_ref[rs, :].astype(BF16), w_up) + b_ref[...]
        la = (jnp.minimum(x, 0.0) - jnp.log(1.0 + jnp.exp(-jnp.abs(x)))) * (1.0 / GLA_TAU)
        sums = _dot(sum_mat, la.astype(BF16))
        cum, tot = sums[:sub], sums[sub:]
        e_tot = jnp.exp(tot)
        et_ref[slot] = e_tot
        qd_ref[slot] = (q_ref[rs, :].astype(F32) * (scale * jnp.exp(cum))).astype(BF16)
        k_in = k_ref[rs, :].astype(F32) * jnp.exp(-cum)
        kst_ref[slot] = (k_in * e_tot).astype(BF16)
        kin_ref[slot] = k_in.astype(BF16)

    def heads(si, slot):
        rs = slice(si * sub, (si + 1) * sub)
        for h in range(GLA_HEADS):
            ks = slice(h * GLA_DK, (h + 1) * GLA_DK)
            vs = slice(h * GLA_DV, (h + 1) * GLA_DV)
            v = v_ref[rs, vs]
            qd = qd_ref[slot, :, ks]
            sc = lax.dot_general(qd, kin_ref[slot, :, ks], NT_DIMS, preferred_element_type=F32)
            o = _dot(jnp.where(tri, sc, 0.0).astype(BF16), v)
            st = st_ref[h]
            inter = [None] * nch
            for ci in (range(nch) if forward else range(nch - 1, -1, -1)):
                cr = slice(ci * c, (ci + 1) * c)
                inter[ci] = lax.dot_general(qd[cr, :], st.astype(BF16), NT_DIMS, preferred_element_type=F32)
                upd = lax.dot_general(v[cr, :], kst_ref[slot, cr, ks], TN_DIMS, preferred_element_type=F32)
                st = st * et_ref[slot, ci * c:ci * c + 1, ks] + upd
            st_ref[h] = st
            o = o + jnp.concatenate(inter, axis=0)
            seq_rows = pl.ds(pl.multiple_of(seq_row0 + si * sub, sub), sub)
            if forward:
                of_ref[seq_rows, vs] = o
            else:
                y = _rms(of_ref[seq_rows, vs] + o, g_ref[...])
                gg = gg_ref[rs, vs].astype(F32)
                o_ref[rs, vs] = (y * (gg * jax.nn.sigmoid(gg))).astype(o_ref.dtype)

    order = list(range(nsub) if forward else range(nsub - 1, -1, -1))
    decayed_operands(order[0], 0)
    for n, si in enumerate(order):
        if n + 1 < nsub:
            decayed_operands(order[n + 1], (n + 1) % 2)
        heads(si, n % 2)


def _gla_kernel(q_ref, k_ref, v_ref, gg_ref, r_ref, wup_ref, b_ref, g_ref,
                o_ref, of_ref, st_ref, *stage_refs, rows, nblk, sub):
    ph = pl.program_id(1)
    i = pl.program_id(2)

    @pl.when(i == 0)
    def _():
        st_ref[...] = jnp.zeros_like(st_ref)

    refs = (q_ref, k_ref, v_ref, gg_ref, r_ref, wup_ref, b_ref, g_ref, o_ref, of_ref, st_ref) + stage_refs

    @pl.when(ph == 0)
    def _():
        _gla_block(1, *refs, rows=rows, sub=sub, seq_row0=i * rows)

    @pl.when(ph == 1)
    def _():
        _gla_block(-1, *refs, rows=rows, sub=sub, seq_row0=(nblk - 1 - i) * rows)


def _gla(layer, proj, rank, wup_pad, b_alpha, g_out, *, batch, seq, rows=1024, sub=128):
    nblk = seq // rows
    t = batch * seq

    def row_block(b, ph, i):
        return b * nblk + jnp.where(ph == 0, i, nblk - 1 - i)

    def parked_block(b, ph, i):
        return b * nblk + jnp.where(ph == 0, nblk - 1, nblk - 1 - i)

    kern = functools.partial(_gla_kernel, rows=rows, nblk=nblk, sub=sub)
    return pl.pallas_call(
        kern,
        grid=(batch, 2, nblk),
        in_specs=[
            pl.BlockSpec((rows, GLA_KEY), lambda b, ph, i: (row_block(b, ph, i), OFF_GQ // GLA_KEY)),
            pl.BlockSpec((rows, GLA_KEY), lambda b, ph, i: (row_block(b, ph, i), OFF_GK // GLA_KEY)),
            pl.BlockSpec((rows, GLA_VAL), lambda b, ph, i: (row_block(b, ph, i), OFF_GV // GLA_VAL)),
            pl.BlockSpec((rows, GLA_VAL), lambda b, ph, i: (parked_block(b, ph, i), OFF_GG // GLA_VAL)),
            pl.BlockSpec((rows, RANK_PAD), lambda b, ph, i: (row_block(b, ph, i), 0)),
            pl.BlockSpec((None, None, RANK_PAD, GLA_KEY), lambda b, ph, i: (layer, ph, 0, 0)),
            pl.BlockSpec((None, None, 1, GLA_KEY), lambda b, ph, i: (layer, ph, 0, 0)),
            _layer_resident(layer, (1, GLA_DV)),
        ],
        out_specs=pl.BlockSpec((rows, GLA_VAL), lambda b, ph, i: (parked_block(b, ph, i), 0)),
        out_shape=jax.ShapeDtypeStruct((t, GLA_VAL), BF16),
        scratch_shapes=[
            pltpu.VMEM((seq, GLA_VAL), F32),
            pltpu.VMEM((GLA_HEADS, GLA_DV, GLA_DK), F32),
            pltpu.VMEM((2, sub, GLA_KEY), BF16),
            pltpu.VMEM((2, sub, GLA_KEY), BF16),
            pltpu.VMEM((2, sub, GLA_KEY), BF16),
            pltpu.VMEM((2, sub, GLA_KEY), F32),
        ],
        compiler_params=pltpu.CompilerParams(
            dimension_semantics=("parallel", "arbitrary", "arbitrary"),
            vmem_limit_bytes=VMEM_LIMIT_BYTES,
        ),
        name="gla",
    )(proj, proj, proj, proj, rank, wup_pad, b_alpha, g_out)


def _attn_kernel(q_ref, k_ref, v_ref, cos_ref, sa_ref, sb_ref, gq_ref, gk_ref,
                 o_ref, kn_ref, v1_ref, m_ref, *s_refs, tq, rt, kc):
    qi = pl.program_id(2)
    nq = pl.num_programs(2)
    seq = kn_ref.shape[0]
    tiles = [(r, hh) for r in range(tq // rt) for hh in range(ATTN_GROUP)]
    assert len(s_refs) == 2 and len(tiles) % 2 == 0
    carried_s_ref = s_refs[(len(tiles) - 1) % 2]

    def rope(x, rows):
        return (x * cos_ref[rows, :]
                + pltpu.roll(x, HEAD_DIM - 32, 1) * sa_ref[rows, :]
                + pltpu.roll(x, 32, 1) * sb_ref[rows, :])

    @pl.when(qi == 0)
    def _():
        k = _rms(k_ref[...].astype(F32), gk_ref[...])
        kn_ref[...] = rope(k, slice(None)).astype(BF16)
        v1_ref[:, :HEAD_DIM] = v_ref[...]
        v1_ref[:, HEAD_DIM:] = jnp.ones((seq, HEAD_DIM), BF16)
        carried_s_ref[...] = jnp.zeros_like(carried_s_ref)
        m_ref[...] = jnp.zeros_like(m_ref)

    scale = HEAD_DIM ** -0.5 * LOG2_E

    def pass1(r, hh, s_ref):
        x = _rms(q_ref[r * rt:(r + 1) * rt, hh * HEAD_DIM:(hh + 1) * HEAD_DIM].astype(F32), gq_ref[...])
        rows = pl.ds(pl.multiple_of(qi * tq + r * rt, rt), rt)
        q = (rope(x, rows) * scale).astype(BF16)
        m = jnp.full((rt, HEAD_DIM), -jnp.inf, F32)
        for c0 in range(0, seq, kc):
            s = lax.dot_general(q, kn_ref[c0:c0 + kc, :], NT_DIMS, preferred_element_type=F32)
            s_ref[:, c0:c0 + kc] = s
            for l0 in range(0, kc, HEAD_DIM):
                m = jnp.maximum(m, s[:, l0:l0 + HEAD_DIM])
        return jnp.max(m, axis=-1, keepdims=True)

    def pass2(row0, hh, s_ref, m):
        acc = jnp.zeros((rt, 2 * HEAD_DIM), F32)
        for c0 in range(0, seq, kc):
            p = jnp.exp2((s_ref[:, c0:c0 + kc] - m).astype(BF16))
            acc = acc + _dot(p, v1_ref[c0:c0 + kc, :])
        o = acc[:, :HEAD_DIM] / acc[:, HEAD_DIM:HEAD_DIM + 1]
        rows = pl.ds(pl.multiple_of(row0, rt), rt)
        o_ref[rows, hh * HEAD_DIM:(hh + 1) * HEAD_DIM] = o.astype(o_ref.dtype)

    last_r, last_hh = tiles[-1]
    prev_block = jnp.where(qi == 0, nq - 1, qi - 1)
    pending = (prev_block * tq + last_r * rt, last_hh, carried_s_ref, m_ref[:, :1])
    for t, (r, hh) in enumerate(tiles):
        s_ref = s_refs[t % 2]
        m = pass1(r, hh, s_ref)
        pass2(*pending)
        pending = (qi * tq + r * rt, hh, s_ref, m)
    m_ref[...] = jnp.broadcast_to(pending[3], m_ref.shape)

    @pl.when(qi == nq - 1)
    def _():
        pass2(*pending)


def _attn(layer, proj, cos_t, sa_t, sb_t, g_q, g_k, *, batch, seq, tq=512, rt=128, kc=512):
    nq = seq // tq
    t = batch * seq
    gw = ATTN_GROUP * HEAD_DIM
    kern = functools.partial(_attn_kernel, tq=tq, rt=rt, kc=kc)
    return pl.pallas_call(
        kern,
        grid=(batch, ATTN_KV_HEADS, nq),
        in_specs=[
            pl.BlockSpec((tq, gw), lambda b, g, i: (b * nq + i, OFF_AQ // gw + g)),
            pl.BlockSpec((seq, HEAD_DIM), lambda b, g, i: (b, OFF_AK // HEAD_DIM + g)),
            pl.BlockSpec((seq, HEAD_DIM), lambda b, g, i: (b, OFF_AV // HEAD_DIM + g)),
            _resident((seq, HEAD_DIM)),
            _resident((seq, HEAD_DIM)),
            _resident((seq, HEAD_DIM)),
            _layer_resident(layer, (1, HEAD_DIM)),
            _layer_resident(layer, (1, HEAD_DIM)),
        ],
        out_specs=pl.BlockSpec((seq, gw), lambda b, g, i: (b, g)),
        out_shape=jax.ShapeDtypeStruct((t, ATTN_Q), BF16),
        scratch_shapes=[
            pltpu.VMEM((seq, HEAD_DIM), BF16),
            pltpu.VMEM((seq, 2 * HEAD_DIM), BF16),
            pltpu.VMEM((rt, HEAD_DIM), F32),
            pltpu.VMEM((rt, seq), F32),
            pltpu.VMEM((rt, seq), F32),
        ],
        compiler_params=pltpu.CompilerParams(
            dimension_semantics=("parallel", "parallel", "arbitrary"),
            vmem_limit_bytes=VMEM_LIMIT_BYTES,
        ),
        name="attn",
    )(proj, proj, proj, cos_t, sa_t, sb_t, g_q, g_k)


def _merge_kernel(a_ref, at_ref, ga_ref, gb_ref, h_ref, woa_ref, wob_ref, wout_ref, g_ref, o_ref,
                  *, sub):
    for r0 in range(0, o_ref.shape[0], sub):
        rs = slice(r0, r0 + sub)
        branch_a = _dot(a_ref[rs, :], woa_ref[...])
        branch_b = _dot(at_ref[rs, :], wob_ref[...])
        mixed = (jax.nn.sigmoid(ga_ref[rs, :].astype(F32)) * branch_a
                 + jax.nn.sigmoid(gb_ref[rs, :].astype(F32)) * branch_b)
        y = _dot(mixed.astype(BF16), wout_ref[...])
        o_ref[rs, :] = h_ref[rs, :] + _rms(y, g_ref[...])


def _merge(layer, gla_out, attn_out, proj, h, w_o_gla, w_o_attn, w_out, g_post, *, tm=1024, sub=256):
    t = h.shape[0]
    row = lambda i: (i, 0)
    return pl.pallas_call(
        functools.partial(_merge_kernel, sub=sub),
        grid=(t // tm,),
        in_specs=[
            pl.BlockSpec((tm, GLA_VAL), row),
            pl.BlockSpec((tm, ATTN_Q), row),
            pl.BlockSpec((tm, D_MODEL), lambda i: (i, OFF_GATE_A // D_MODEL)),
            pl.BlockSpec((tm, D_MODEL), lambda i: (i, OFF_GATE_B // D_MODEL)),
            pl.BlockSpec((tm, D_MODEL), row),
            _layer_resident(layer, (GLA_VAL, D_MODEL)),
            _layer_resident(layer, (ATTN_Q, D_MODEL)),
            _layer_resident(layer, (D_MODEL, D_MODEL)),
            _layer_resident(layer, (1, D_MODEL)),
        ],
        out_specs=pl.BlockSpec((tm, D_MODEL), row),
        out_shape=jax.ShapeDtypeStruct((t, D_MODEL), F32),
        compiler_params=pltpu.CompilerParams(
            dimension_semantics=("parallel",),
            vmem_limit_bytes=VMEM_LIMIT_BYTES,
        ),
        name="merge",
    )(gla_out, attn_out, proj, proj, h, w_o_gla, w_o_attn, w_out, g_post)


def _ffn_kernel(h_ref, p_ref, gpre_ref, win_ref, wout_ref, gpost_ref, wpp_ref, wpg_ref, gple_ref,
                o_ref, *, hidden_chunk, sub):
    for r0 in range(0, o_ref.shape[0], sub):
        rs = slice(r0, r0 + sub)
        h = h_ref[rs, :]
        x = _rms(h, gpre_ref[...]).astype(BF16)
        acc = jnp.zeros(h.shape, F32)
        for c0 in range(0, FFN_HIDDEN, hidden_chunk):
            gate = _dot(x, win_ref[:, c0:c0 + hidden_chunk])
            up = _dot(x, win_ref[:, FFN_HIDDEN + c0:FFN_HIDDEN + c0 + hidden_chunk])
            act = (gate * jax.nn.sigmoid(gate)) * up
            acc = acc + _dot(act.astype(BF16), wout_ref[c0:c0 + hidden_chunk, :])
        h = h + _rms(acc, gpost_ref[...])
        e = _dot(p_ref[rs, :].astype(BF16), wpp_ref[...])
        gate = jax.nn.sigmoid(_dot(h.astype(BF16), wpg_ref[...]))
        o_ref[rs, :] = h + _rms(gate * e, gple_ref[...])


def _ffn(layer, h, p, g_pre, w_in, w_out, g_post, w_pp, w_pg, g_ple, *, seq, tm=1024, sub=256,
         hidden_chunk=FFN_HIDDEN):
    t = h.shape[0]
    row = lambda i: (i, 0)
    tiles_per_seq = seq // tm
    kern = functools.partial(_ffn_kernel, hidden_chunk=hidden_chunk, sub=sub)
    return pl.pallas_call(
        kern,
        grid=(t // tm,),
        in_specs=[
            pl.BlockSpec((tm, D_MODEL), row),
            pl.BlockSpec((None, None, tm, PLE_DIM),
                         lambda i: (layer, i // tiles_per_seq, i % tiles_per_seq, 0)),
            _layer_resident(layer, (1, D_MODEL)),
            _layer_resident(layer, (D_MODEL, 2 * FFN_HIDDEN)),
            _layer_resident(layer, (FFN_HIDDEN, D_MODEL)),
            _layer_resident(layer, (1, D_MODEL)),
            _layer_resident(layer, (PLE_DIM, D_MODEL)),
            _layer_resident(layer, (D_MODEL, D_MODEL)),
            _layer_resident(layer, (1, D_MODEL)),
        ],
        out_specs=pl.BlockSpec((tm, D_MODEL), row),
        out_shape=jax.ShapeDtypeStruct((t, D_MODEL), F32),
        compiler_params=pltpu.CompilerParams(
            dimension_semantics=("parallel",),
            vmem_limit_bytes=VMEM_LIMIT_BYTES,
        ),
        name="ffn",
    )(h, p, g_pre, w_in, w_out, g_post, w_pp, w_pg, g_ple)


def _rope_tables(seq):
    rows = seq // GRID_W
    row = jnp.repeat(jnp.arange(rows, dtype=F32), GRID_W)
    col = jnp.tile(jnp.arange(GRID_W, dtype=F32), rows)
    inv = ROPE_THETA ** (-jnp.arange(0, ROPE_AXIS_DIM, 2, dtype=F32) / ROPE_AXIS_DIM)
    ang = jnp.stack([row[:, None] * inv, col[:, None] * inv], axis=1)
    cos, sin = jnp.cos(ang), jnp.sin(ang)
    zero = jnp.zeros_like(sin)
    lanes = lambda a, b: jnp.stack([a, b], axis=2).reshape(seq, HEAD_DIM)
    return lanes(cos, cos), lanes(-sin, zero), lanes(zero, sin)


def _permute_w_in(w_in):
    sizes = (GLA_KEY, GLA_KEY, GLA_VAL, GLA_VAL, GLA_RANK, GLA_RANK,
             ATTN_Q, ATTN_KV, ATTN_KV, D_MODEL, D_MODEL)
    offs = np.cumsum((0,) + sizes)
    gq, gk, gv, gg, ra_f, ra_b, aq, ak, av, gate_a, gate_b = (
        w_in[:, :, offs[n]:offs[n + 1]] for n in range(len(sizes)))
    main = jnp.concatenate([gv, gg, aq, gate_a, gate_b, gq, gk, ak, av], axis=2)
    rank = jnp.concatenate([ra_f, ra_b], axis=2)
    rank = jnp.pad(rank, ((0, 0), (0, 0), (0, RANK_PAD - 2 * GLA_RANK)))
    return main.astype(BF16), rank.astype(BF16)


def _pad_w_alpha(w_alpha_up):
    pads = [((0, 0), (d * GLA_RANK, RANK_PAD - (d + 1) * GLA_RANK), (0, 0)) for d in range(2)]
    return jnp.stack([jnp.pad(w_alpha_up[:, d], pads[d]) for d in range(2)], axis=1)


def kernel(x, p, g_mix_pre, w_in, w_alpha_up, b_alpha, g_gla_out, g_q_norm, g_k_norm, w_o_gla,
           w_o_attn, w_out, g_mix_post, g_ffn_pre, w_ffn_in, w_ffn_out, g_ffn_post, w_ple_proj,
           w_ple_gate, g_ple_post):
    batch, seq, _ = x.shape
    depth = p.shape[0]
    t = batch * seq
    cos_t, sa_t, sb_t = _rope_tables(seq)
    gain = lambda g: g.reshape(depth, 1, -1)
    w_main, w_rank = _permute_w_in(w_in)
    wup_pad = _pad_w_alpha(w_alpha_up)
    b_alpha = b_alpha.reshape(depth, 2, 1, GLA_KEY)
    w_o_gla, w_o_attn, w_out, w_ffn_in, w_ffn_out, w_ple_proj, w_ple_gate = (
        w.astype(BF16) for w in (w_o_gla, w_o_attn, w_out, w_ffn_in, w_ffn_out, w_ple_proj, w_ple_gate))
    h = x.reshape(t, D_MODEL)
    for i in range(depth):
        proj, rank = _proj(i, h, gain(g_mix_pre), w_main, w_rank)
        gla_out = _gla(i, proj, rank, wup_pad, b_alpha, gain(g_gla_out), batch=batch, seq=seq)
        attn_out = _attn(i, proj, cos_t, sa_t, sb_t, gain(g_q_norm), gain(g_k_norm),
                         batch=batch, seq=seq)
        h = _merge(i, gla_out, attn_out, proj, h, w_o_gla, w_o_attn, w_out, gain(g_mix_post))
        h = _ffn(i, h, p, gain(g_ffn_pre), w_ffn_in, w_ffn_out, gain(g_ffn_post), w_ple_proj,
                 w_ple_gate, gain(g_ple_post), seq=seq)
    return h.reshape(batch, seq, D_MODEL)
```

```python
import functools

import jax
import jax.numpy as jnp
import numpy as np
from jax import lax
from jax.experimental import pallas as pl
from jax.experimental.pallas import tpu as pltpu

F32 = jnp.float32
BF16 = jnp.bfloat16

D_MODEL = 1024
GRID_W = 64
PLE_DIM = 256
EPS = 1e-6
GLA_HEADS = 4
GLA_DK = 128
GLA_DV = 256
GLA_KEY = GLA_HEADS * GLA_DK
GLA_VAL = GLA_HEADS * GLA_DV
GLA_RANK = 16
GLA_TAU = 16.0
GLA_CHUNK = 64
GLA_CHUNK_SHIFT = 6
assert 1 << GLA_CHUNK_SHIFT == GLA_CHUNK
ATTN_HEADS = 8
ATTN_KV_HEADS = 2
ATTN_GROUP = ATTN_HEADS // ATTN_KV_HEADS
HEAD_DIM = 128
ATTN_Q = ATTN_HEADS * HEAD_DIM
ATTN_KV = ATTN_KV_HEADS * HEAD_DIM
ROPE_THETA = 10000.0
ROPE_AXIS_DIM = HEAD_DIM // 2
FFN_HIDDEN = 2816
LOG2_E = 1.4426950408889634

OFF_GV = 0
OFF_GG = 1024
OFF_AQ = 2048
OFF_GATE_A = 3072
OFF_GATE_B = 4096
OFF_GQ = 5120
OFF_GK = 5632
OFF_AK = 6144
OFF_AV = 6400
PROJ_WIDTH = 6656
RANK_PAD = 128

VMEM_LIMIT_BYTES = 56 * 1024 * 1024

NT_DIMS = (((1,), (1,)), ((), ()))
TN_DIMS = (((0,), (0,)), ((), ()))


def _rms(x, g):
    return x * lax.rsqrt(jnp.mean(x * x, axis=-1, keepdims=True) + EPS) * g


def _dot(a, b):
    return jnp.dot(a, b, preferred_element_type=F32)


def _resident(shape):
    zeros = (0,) * len(shape)
    return pl.BlockSpec(shape, lambda *_: zeros, pipeline_mode=pl.Buffered(1))


def _layer_resident(layer, shape):
    index = (layer,) + (0,) * len(shape)
    return pl.BlockSpec((None,) + shape, lambda *_: index, pipeline_mode=pl.Buffered(1))


def _proj_kernel(x_ref, g_ref, w_ref, wr_ref, o_ref, r_ref, *, tn):
    u = _rms(x_ref[...], g_ref[...]).astype(BF16)
    r_ref[...] = _dot(u, wr_ref[...])
    for n0 in range(0, w_ref.shape[1], tn):
        o_ref[:, n0:n0 + tn] = _dot(u, w_ref[:, n0:n0 + tn]).astype(o_ref.dtype)


def _proj(layer, h, g, w, w_rank, *, tm=512, tn=512):
    t = h.shape[0]
    n = w.shape[2]
    row = lambda i: (i, 0)
    return pl.pallas_call(
        functools.partial(_proj_kernel, tn=tn),
        grid=(t // tm,),
        in_specs=[
            pl.BlockSpec((tm, D_MODEL), row),
            _layer_resident(layer, (1, D_MODEL)),
            _layer_resident(layer, (D_MODEL, n)),
            _layer_resident(layer, (D_MODEL, RANK_PAD)),
        ],
        out_specs=[
            pl.BlockSpec((tm, n), row),
            pl.BlockSpec((tm, RANK_PAD), row),
        ],
        out_shape=[
            jax.ShapeDtypeStruct((t, n), BF16),
            jax.ShapeDtypeStruct((t, RANK_PAD), F32),
        ],
        compiler_params=pltpu.CompilerParams(
            dimension_semantics=("parallel",),
            vmem_limit_bytes=VMEM_LIMIT_BYTES,
        ),
        name="proj",
    )(h, g, w, w_rank)


def _gla_block(direction, q_ref, k_ref, v_ref, gg_ref, r_ref, wup_ref, b_ref, g_ref,
               o_ref, of_ref, st_ref, qd_ref, kin_ref, kst_ref, et_ref, *, rows, sub, seq_row0):
    c = GLA_CHUNK
    nsub = rows // sub
    nch = sub // c
    forward = direction > 0
    row = lax.broadcasted_iota(jnp.int32, (sub, sub), 0)
    col = lax.broadcasted_iota(jnp.int32, (sub, sub), 1)
    same = (row >> GLA_CHUNK_SHIFT) == (col >> GLA_CHUNK_SHIFT)
    tri = jnp.logical_and(same, (col <= row) if forward else (col >= row))
    sum_mat = jnp.concatenate([jnp.where(tri, 1.0, 0.0), jnp.where(same, 1.0, 0.0)], axis=0).astype(BF16)
    w_up = wup_ref[...].astype(BF16)
    scale = GLA_DK ** -0.5

    def decayed_operands(si, slot):
        rs = slice(si * sub, (si + 1) * sub)
        x = _dot(r_ref[rs, :].astype(BF16), w_up) + b_ref[...]
        la = (jnp.minimum(x, 0.0) - jnp.log(1.0 + jnp.exp(-jnp.abs(x)))) * (1.0 / GLA_TAU)
        sums = _dot(sum_mat, la.astype(BF16))
        cum, tot = sums[:sub], sums[sub:]
        e_tot = jnp.exp(tot)
        et_ref[slot] = e_tot
        qd_ref[slot] = (q_ref[rs, :].astype(F32) * (scale * jnp.exp(cum))).astype(BF16)
        k_in = k_ref[rs, :].astype(F32) * jnp.exp(-cum)
        kst_ref[slot] = (k_in * e_tot).astype(BF16)
        kin_ref[slot] = k_in.astype(BF16)

    def heads(si, slot):
        rs = slice(si * sub, (si + 1) * sub)
        for h in range(GLA_HEADS):
            ks = slice(h * GLA_DK, (h + 1) * GLA_DK)
            vs = slice(h * GLA_DV, (h + 1) * GLA_DV)
            v = v_ref[rs, vs]
            qd = qd_ref[slot, :, ks]
            sc = lax.dot_general(qd, kin_ref[slot, :, ks], NT_DIMS, preferred_element_type=F32)
            o = _dot(jnp.where(tri, sc, 0.0).astype(BF16), v)
            st = st_ref[h]
            inter = [None] * nch
            for ci in (range(nch) if forward else range(nch - 1, -1, -1)):
                cr = slice(ci * c, (ci + 1) * c)
                inter[ci] = lax.dot_general(qd[cr, :], st.astype(BF16), NT_DIMS, preferred_element_type=F32)
                upd = lax.dot_general(v[cr, :], kst_ref[slot, cr, ks], TN_DIMS, preferred_element_type=F32)
                st = st * et_ref[slot, ci * c:ci * c + 1, ks] + upd
            st_ref[h] = st
            o = o + jnp.concatenate(inter, axis=0)
            seq_rows = pl.ds(pl.multiple_of(seq_row0 + si * sub, sub), sub)
            if forward:
                of_ref[seq_rows, vs] = o
            else:
                y = _rms(of_ref[seq_rows, vs] + o, g_ref[...])
                gg = gg_ref[rs, vs].astype(F32)
                o_ref[rs, vs] = (y * (gg * jax.nn.sigmoid(gg))).astype(o_ref.dtype)

    order = list(range(nsub) if forward else range(nsub - 1, -1, -1))
    decayed_operands(order[0], 0)
    for n, si in enumerate(order):
        if n + 1 < nsub:
            decayed_operands(order[n + 1], (n + 1) % 2)
        heads(si, n % 2)


def _gla_kernel(q_ref, k_ref, v_ref, gg_ref, r_ref, wup_ref, b_ref, g_ref,
                o_ref, of_ref, st_ref, *stage_refs, rows, nblk, sub):
    ph = pl.program_id(1)
    i = pl.program_id(2)

    @pl.when(i == 0)
    def _():
        st_ref[...] = jnp.zeros_like(st_ref)

    refs = (q_ref, k_ref, v_ref, gg_ref, r_ref, wup_ref, b_ref, g_ref, o_ref, of_ref, st_ref) + stage_refs

    @pl.when(ph == 0)
    def _():
        _gla_block(1, *refs, rows=rows, sub=sub, seq_row0=i * rows)

    @pl.when(ph == 1)
    def _():
        _gla_block(-1, *refs, rows=rows, sub=sub, seq_row0=(nblk - 1 - i) * rows)


def _gla(layer, proj, rank, wup_pad, b_alpha, g_out, *, batch, seq, rows=1024, sub=128):
    nblk = seq // rows
    t = batch * seq

    def row_block(b, ph, i):
        return b * nblk + jnp.where(ph == 0, i, nblk - 1 - i)

    def parked_block(b, ph, i):
        return b * nblk + jnp.where(ph == 0, nblk - 1, nblk - 1 - i)

    kern = functools.partial(_gla_kernel, rows=rows, nblk=nblk, sub=sub)
    return pl.pallas_call(
        kern,
        grid=(batch, 2, nblk),
        in_specs=[
            pl.BlockSpec((rows, GLA_KEY), lambda b, ph, i: (row_block(b, ph, i), OFF_GQ // GLA_KEY)),
            pl.BlockSpec((rows, GLA_KEY), lambda b, ph, i: (row_block(b, ph, i), OFF_GK // GLA_KEY)),
            pl.BlockSpec((rows, GLA_VAL), lambda b, ph, i: (row_block(b, ph, i), OFF_GV // GLA_VAL)),
            pl.BlockSpec((rows, GLA_VAL), lambda b, ph, i: (parked_block(b, ph, i), OFF_GG // GLA_VAL)),
            pl.BlockSpec((rows, RANK_PAD), lambda b, ph, i: (row_block(b, ph, i), 0)),
            pl.BlockSpec((None, None, RANK_PAD, GLA_KEY), lambda b, ph, i: (layer, ph, 0, 0)),
            pl.BlockSpec((None, None, 1, GLA_KEY), lambda b, ph, i: (layer, ph, 0, 0)),
            _layer_resident(layer, (1, GLA_DV)),
        ],
        out_specs=pl.BlockSpec((rows, GLA_VAL), lambda b, ph, i: (parked_block(b, ph, i), 0)),
        out_shape=jax.ShapeDtypeStruct((t, GLA_VAL), BF16),
        scratch_shapes=[
            pltpu.VMEM((seq, GLA_VAL), F32),
            pltpu.VMEM((GLA_HEADS, GLA_DV, GLA_DK), F32),
            pltpu.VMEM((2, sub, GLA_KEY), BF16),
            pltpu.VMEM((2, sub, GLA_KEY), BF16),
            pltpu.VMEM((2, sub, GLA_KEY), BF16),
            pltpu.VMEM((2, sub, GLA_KEY), F32),
        ],
        compiler_params=pltpu.CompilerParams(
            dimension_semantics=("parallel", "arbitrary", "arbitrary"),
            vmem_limit_bytes=VMEM_LIMIT_BYTES,
        ),
        name="gla",
    )(proj, proj, proj, proj, rank, wup_pad, b_alpha, g_out)


def _attn_kernel(q_ref, k_ref, v_ref, cos_ref, sa_ref, sb_ref, gq_ref, gk_ref,
                 o_ref, kn_ref, v1_ref, m_ref, *s_refs, tq, rt, kc):
    qi = pl.program_id(2)
    nq = pl.num_programs(2)
    seq = kn_ref.shape[0]
    tiles = [(r, hh) for r in range(tq // rt) for hh in range(ATTN_GROUP)]
    assert len(s_refs) == 2 and len(tiles) % 2 == 0
    carried_s_ref = s_refs[(len(tiles) - 1) % 2]

    def rope(x, rows):
        return (x * cos_ref[rows, :]
                + pltpu.roll(x, HEAD_DIM - 32, 1) * sa_ref[rows, :]
                + pltpu.roll(x, 32, 1) * sb_ref[rows, :])

    @pl.when(qi == 0)
    def _():
        k = _rms(k_ref[...].astype(F32), gk_ref[...])
        kn_ref[...] = rope(k, slice(None)).astype(BF16)
        v1_ref[:, :HEAD_DIM] = v_ref[...]
        v1_ref[:, HEAD_DIM:] = jnp.ones((seq, HEAD_DIM), BF16)
        carried_s_ref[...] = jnp.zeros_like(carried_s_ref)
        m_ref[...] = jnp.zeros_like(m_ref)

    scale = HEAD_DIM ** -0.5 * LOG2_E

    def pass1(r, hh, s_ref):
        x = _rms(q_ref[r * rt:(r + 1) * rt, hh * HEAD_DIM:(hh + 1) * HEAD_DIM].astype(F32), gq_ref[...])
        rows = pl.ds(pl.multiple_of(qi * tq + r * rt, rt), rt)
        q = (rope(x, rows) * scale).astype(BF16)
        m = jnp.full((rt, HEAD_DIM), -jnp.inf, F32)
        for c0 in range(0, seq, kc):
            s = lax.dot_general(q, kn_ref[c0:c0 + kc, :], NT_DIMS, preferred_element_type=F32)
            s_ref[:, c0:c0 + kc] = s
            for l0 in range(0, kc, HEAD_DIM):
                m = jnp.maximum(m, s[:, l0:l0 + HEAD_DIM])
        return jnp.max(m, axis=-1, keepdims=True)

    def pass2(row0, hh, s_ref, m):
        acc = jnp.zeros((rt, 2 * HEAD_DIM), F32)
        for c0 in range(0, seq, kc):
            p = jnp.exp2((s_ref[:, c0:c0 + kc] - m).astype(BF16))
            acc = acc + _dot(p, v1_ref[c0:c0 + kc, :])
        o = acc[:, :HEAD_DIM] / acc[:, HEAD_DIM:HEAD_DIM + 1]
        rows = pl.ds(pl.multiple_of(row0, rt), rt)
        o_ref[rows, hh * HEAD_DIM:(hh + 1) * HEAD_DIM] = o.astype(o_ref.dtype)

    last_r, last_hh = tiles[-1]
    prev_block = jnp.where(qi == 0, nq - 1, qi - 1)
    pending = (prev_block * tq + last_r * rt, last_hh, carried_s_ref, m_ref[:, :1])
    for t, (r, hh) in enumerate(tiles):
        s_ref = s_refs[t % 2]
        m = pass1(r, hh, s_ref)
        pass2(*pending)
        pending = (qi * tq + r * rt, hh, s_ref, m)
    m_ref[...] = jnp.broadcast_to(pending[3], m_ref.shape)

    @pl.when(qi == nq - 1)
    def _():
        pass2(*pending)


def _attn(layer, proj, cos_t, sa_t, sb_t, g_q, g_k, *, batch, seq, tq=1024, rt=128, kc=512):
    nq = seq // tq
    t = batch * seq
    gw = ATTN_GROUP * HEAD_DIM
    kern = functools.partial(_attn_kernel, tq=tq, rt=rt, kc=kc)
    return pl.pallas_call(
        kern,
        grid=(batch, ATTN_KV_HEADS, nq),
        in_specs=[
            pl.BlockSpec((tq, gw), lambda b, g, i: (b * nq + i, OFF_AQ // gw + g)),
            pl.BlockSpec((seq, HEAD_DIM), lambda b, g, i: (b, OFF_AK // HEAD_DIM + g)),
            pl.BlockSpec((seq, HEAD_DIM), lambda b, g, i: (b, OFF_AV // HEAD_DIM + g)),
            _resident((seq, HEAD_DIM)),
            _resident((seq, HEAD_DIM)),
            _resident((seq, HEAD_DIM)),
            _layer_resident(layer, (1, HEAD_DIM)),
            _layer_resident(layer, (1, HEAD_DIM)),
        ],
        out_specs=pl.BlockSpec((seq, gw), lambda b, g, i: (b, g)),
        out_shape=jax.ShapeDtypeStruct((t, ATTN_Q), BF16),
        scratch_shapes=[
            pltpu.VMEM((seq, HEAD_DIM), BF16),
            pltpu.VMEM((seq, 2 * HEAD_DIM), BF16),
            pltpu.VMEM((rt, HEAD_DIM), F32),
            pltpu.VMEM((rt, seq), F32),
            pltpu.VMEM((rt, seq), F32),
        ],
        compiler_params=pltpu.CompilerParams(
            dimension_semantics=("parallel", "parallel", "arbitrary"),
            vmem_limit_bytes=VMEM_LIMIT_BYTES,
        ),
        name="attn",
    )(proj, proj, proj, cos_t, sa_t, sb_t, g_q, g_k)


def _merge_kernel(a_ref, at_ref, ga_ref, gb_ref, h_ref, woa_ref, wob_ref, wout_ref, g_ref, o_ref,
                  *, sub):
    for r0 in range(0, o_ref.shape[0], sub):
        rs = slice(r0, r0 + sub)
        branch_a = _dot(a_ref[rs, :], woa_ref[...])
        branch_b = _dot(at_ref[rs, :], wob_ref[...])
        mixed = (jax.nn.sigmoid(ga_ref[rs, :].astype(F32)) * branch_a
                 + jax.nn.sigmoid(gb_ref[rs, :].astype(F32)) * branch_b)
        y = _dot(mixed.astype(BF16), wout_ref[...])
        o_ref[rs, :] = h_ref[rs, :] + _rms(y, g_ref[...])


def _merge(layer, gla_out, attn_out, proj, h, w_o_gla, w_o_attn, w_out, g_post, *, tm=1024, sub=256):
    t = h.shape[0]
    row = lambda i: (i, 0)
    return pl.pallas_call(
        functools.partial(_merge_kernel, sub=sub),
        grid=(t // tm,),
        in_specs=[
            pl.BlockSpec((tm, GLA_VAL), row),
            pl.BlockSpec((tm, ATTN_Q), row),
            pl.BlockSpec((tm, D_MODEL), lambda i: (i, OFF_GATE_A // D_MODEL)),
            pl.BlockSpec((tm, D_MODEL), lambda i: (i, OFF_GATE_B // D_MODEL)),
            pl.BlockSpec((tm, D_MODEL), row),
            _layer_resident(layer, (GLA_VAL, D_MODEL)),
            _layer_resident(layer, (ATTN_Q, D_MODEL)),
            _layer_resident(layer, (D_MODEL, D_MODEL)),
            _layer_resident(layer, (1, D_MODEL)),
        ],
        out_specs=pl.BlockSpec((tm, D_MODEL), row),
        out_shape=jax.ShapeDtypeStruct((t, D_MODEL), F32),
        compiler_params=pltpu.CompilerParams(
            dimension_semantics=("parallel",),
            vmem_limit_bytes=VMEM_LIMIT_BYTES,
        ),
        name="merge",
    )(gla_out, attn_out, proj, proj, h, w_o_gla, w_o_attn, w_out, g_post)


def _ffn_kernel(h_ref, p_ref, gpre_ref, win_ref, wout_ref, gpost_ref, wpp_ref, wpg_ref, gple_ref,
                o_ref, *, hidden_chunk, sub):
    for r0 in range(0, o_ref.shape[0], sub):
        rs = slice(r0, r0 + sub)
        h = h_ref[rs, :]
        x = _rms(h, gpre_ref[...]).astype(BF16)
        acc = jnp.zeros(h.shape, F32)
        for c0 in range(0, FFN_HIDDEN, hidden_chunk):
            gate = _dot(x, win_ref[:, c0:c0 + hidden_chunk])
            up = _dot(x, win_ref[:, FFN_HIDDEN + c0:FFN_HIDDEN + c0 + hidden_chunk])
            act = (gate * jax.nn.sigmoid(gate)) * up
            acc = acc + _dot(act.astype(BF16), wout_ref[c0:c0 + hidden_chunk, :])
        h = h + _rms(acc, gpost_ref[...])
        e = _dot(p_ref[rs, :].astype(BF16), wpp_ref[...])
        gate = jax.nn.sigmoid(_dot(h.astype(BF16), wpg_ref[...]))
        o_ref[rs, :] = h + _rms(gate * e, gple_ref[...])


def _ffn(layer, h, p, g_pre, w_in, w_out, g_post, w_pp, w_pg, g_ple, *, seq, tm=1024, sub=256,
         hidden_chunk=FFN_HIDDEN):
    t = h.shape[0]
    row = lambda i: (i, 0)
    tiles_per_seq = seq // tm
    kern = functools.partial(_ffn_kernel, hidden_chunk=hidden_chunk, sub=sub)
    return pl.pallas_call(
        kern,
        grid=(t // tm,),
        in_specs=[
            pl.BlockSpec((tm, D_MODEL), row),
            pl.BlockSpec((None, None, tm, PLE_DIM),
                         lambda i: (layer, i // tiles_per_seq, i % tiles_per_seq, 0)),
            _layer_resident(layer, (1, D_MODEL)),
            _layer_resident(layer, (D_MODEL, 2 * FFN_HIDDEN)),
            _layer_resident(layer, (FFN_HIDDEN, D_MODEL)),
            _layer_resident(layer, (1, D_MODEL)),
            _layer_resident(layer, (PLE_DIM, D_MODEL)),
            _layer_resident(layer, (D_MODEL, D_MODEL)),
            _layer_resident(layer, (1, D_MODEL)),
        ],
        out_specs=pl.BlockSpec((tm, D_MODEL), row),
        out_shape=jax.ShapeDtypeStruct((t, D_MODEL), F32),
        compiler_params=pltpu.CompilerParams(
            dimension_semantics=("parallel",),
            vmem_limit_bytes=VMEM_LIMIT_BYTES,
        ),
        name="ffn",
    )(h, p, g_pre, w_in, w_out, g_post, w_pp, w_pg, g_ple)


def _rope_tables(seq):
    rows = seq // GRID_W
    row = jnp.repeat(jnp.arange(rows, dtype=F32), GRID_W)
    col = jnp.tile(jnp.arange(GRID_W, dtype=F32), rows)
    inv = ROPE_THETA ** (-jnp.arange(0, ROPE_AXIS_DIM, 2, dtype=F32) / ROPE_AXIS_DIM)
    ang = jnp.stack([row[:, None] * inv, col[:, None] * inv], axis=1)
    cos, sin = jnp.cos(ang), jnp.sin(ang)
    zero = jnp.zeros_like(sin)
    lanes = lambda a, b: jnp.stack([a, b], axis=2).reshape(seq, HEAD_DIM)
    return lanes(cos, cos), lanes(-sin, zero), lanes(zero, sin)


def _permute_w_in(w_in):
    sizes = (GLA_KEY, GLA_KEY, GLA_VAL, GLA_VAL, GLA_RANK, GLA_RANK,
             ATTN_Q, ATTN_KV, ATTN_KV, D_MODEL, D_MODEL)
    offs = np.cumsum((0,) + sizes)
    gq, gk, gv, gg, ra_f, ra_b, aq, ak, av, gate_a, gate_b = (
        w_in[:, :, offs[n]:offs[n + 1]] for n in range(len(sizes)))
    main = jnp.concatenate([gv, gg, aq, gate_a, gate_b, gq, gk, ak, av], axis=2)
    rank = jnp.concatenate([ra_f, ra_b], axis=2)
    rank = jnp.pad(rank, ((0, 0), (0, 0), (0, RANK_PAD - 2 * GLA_RANK)))
    return main.astype(BF16), rank.astype(BF16)


def _pad_w_alpha(w_alpha_up):
    pads = [((0, 0), (d * GLA_RANK, RANK_PAD - (d + 1) * GLA_RANK), (0, 0)) for d in range(2)]
    return jnp.stack([jnp.pad(w_alpha_up[:, d], pads[d]) for d in range(2)], axis=1)


def kernel(x, p, g_mix_pre, w_in, w_alpha_up, b_alpha, g_gla_out, g_q_norm, g_k_norm, w_o_gla,
           w_o_attn, w_out, g_mix_post, g_ffn_pre, w_ffn_in, w_ffn_out, g_ffn_post, w_ple_proj,
           w_ple_gate, g_ple_post):
    batch, seq, _ = x.shape
    depth = p.shape[0]
    t = batch * seq
    cos_t, sa_t, sb_t = _rope_tables(seq)
    gain = lambda g: g.reshape(depth, 1, -1)
    w_main, w_rank = _permute_w_in(w_in)
    wup_pad = _pad_w_alpha(w_alpha_up)
    b_alpha = b_alpha.reshape(depth, 2, 1, GLA_KEY)
    w_o_gla, w_o_attn, w_out, w_ffn_in, w_ffn_out, w_ple_proj, w_ple_gate = (
        w.astype(BF16) for w in (w_o_gla, w_o_attn, w_out, w_ffn_in, w_ffn_out, w_ple_proj, w_ple_gate))
    h = x.reshape(t, D_MODEL)
    for i in range(depth):
        proj, rank = _proj(i, h, gain(g_mix_pre), w_main, w_rank)
        gla_out = _gla(i, proj, rank, wup_pad, b_alpha, gain(g_gla_out), batch=batch, seq=seq)
        attn_out = _attn(i, proj, cos_t, sa_t, sb_t, gain(g_q_norm), gain(g_k_norm),
                         batch=batch, seq=seq)
        h = _merge(i, gla_out, attn_out, proj, h, w_o_gla, w_o_attn, w_out, gain(g_mix_post))
        h = _ffn(i, h, p, gain(g_ffn_pre), w_ffn_in, w_ffn_out, gain(g_ffn_post), w_ple_proj,
                 w_ple_gate, gain(g_ple_post), seq=seq)
    return h.reshape(batch, seq, D_MODEL)
```

```python
import functools

import jax
import jax.numpy as jnp
import numpy as np
from jax import lax
from jax.experimental import pallas as pl
from jax.experimental.pallas import tpu as pltpu

F32 = jnp.float32
BF16 = jnp.bfloat16

D_MODEL = 1024
GRID_W = 64
PLE_DIM = 256
EPS = 1e-6
GLA_HEADS = 4
GLA_DK = 128
GLA_DV = 256
GLA_KEY = GLA_HEADS * GLA_DK
GLA_VAL = GLA_HEADS * GLA_DV
GLA_RANK = 16
GLA_TAU = 16.0
GLA_CHUNK = 64
GLA_CHUNK_SHIFT = 6
assert 1 << GLA_CHUNK_SHIFT == GLA_CHUNK
ATTN_HEADS = 8
ATTN_KV_HEADS = 2
ATTN_GROUP = ATTN_HEADS // ATTN_KV_HEADS
HEAD_DIM = 128
ATTN_Q = ATTN_HEADS * HEAD_DIM
ATTN_KV = ATTN_KV_HEADS * HEAD_DIM
ROPE_THETA = 10000.0
ROPE_AXIS_DIM = HEAD_DIM // 2
FFN_HIDDEN = 2816
LOG2_E = 1.4426950408889634

OFF_GV = 0
OFF_GG = 1024
OFF_AQ = 2048
OFF_GATE_A = 3072
OFF_GATE_B = 4096
OFF_GQ = 5120
OFF_GK = 5632
OFF_AK = 6144
OFF_AV = 6400
PROJ_WIDTH = 6656
RANK_PAD = 128

VMEM_LIMIT_BYTES = 56 * 1024 * 1024

NT_DIMS = (((1,), (1,)), ((), ()))
TN_DIMS = (((0,), (0,)), ((), ()))


def _rms(x, g):
    return x * lax.rsqrt(jnp.mean(x * x, axis=-1, keepdims=True) + EPS) * g


def _dot(a, b):
    return jnp.dot(a, b, preferred_element_type=F32)


def _resident(shape):
    zeros = (0,) * len(shape)
    return pl.BlockSpec(shape, lambda *_: zeros, pipeline_mode=pl.Buffered(1))


def _layer_resident(layer, shape):
    index = (layer,) + (0,) * len(shape)
    return pl.BlockSpec((None,) + shape, lambda *_: index, pipeline_mode=pl.Buffered(1))


def _proj_kernel(x_ref, g_ref, w_ref, wr_ref, o_ref, r_ref, *, tn):
    u = _rms(x_ref[...], g_ref[...]).astype(BF16)
    r_ref[...] = _dot(u, wr_ref[...])
    for n0 in range(0, w_ref.shape[1], tn):
        o_ref[:, n0:n0 + tn] = _dot(u, w_ref[:, n0:n0 + tn]).astype(o_ref.dtype)


def _proj(layer, h, g, w, w_rank, *, tm=512, tn=512):
    t = h.shape[0]
    n = w.shape[2]
    row = lambda i: (i, 0)
    return pl.pallas_call(
        functools.partial(_proj_kernel, tn=tn),
        grid=(t // tm,),
        in_specs=[
            pl.BlockSpec((tm, D_MODEL), row),
            _layer_resident(layer, (1, D_MODEL)),
            _layer_resident(layer, (D_MODEL, n)),
            _layer_resident(layer, (D_MODEL, RANK_PAD)),
        ],
        out_specs=[
            pl.BlockSpec((tm, n), row),
            pl.BlockSpec((tm, RANK_PAD), row),
        ],
        out_shape=[
            jax.ShapeDtypeStruct((t, n), BF16),
            jax.ShapeDtypeStruct((t, RANK_PAD), F32),
        ],
        compiler_params=pltpu.CompilerParams(
            dimension_semantics=("parallel",),
            vmem_limit_bytes=VMEM_LIMIT_BYTES,
        ),
        name="proj",
    )(h, g, w, w_rank)


def _gla_block(direction, q_ref, k_ref, v_ref, gg_ref, r_ref, wup_ref, b_ref, g_ref,
               o_ref, of_ref, st_ref, qd_ref, kin_ref, kst_ref, et_ref, *, rows, sub, seq_row0):
    c = GLA_CHUNK
    nsub = rows // sub
    nch = sub // c
    forward = direction > 0
    row = lax.broadcasted_iota(jnp.int32, (sub, sub), 0)
    col = lax.broadcasted_iota(jnp.int32, (sub, sub), 1)
    same = (row >> GLA_CHUNK_SHIFT) == (col >> GLA_CHUNK_SHIFT)
    tri = jnp.logical_and(same, (col <= row) if forward else (col >= row))
    sum_mat = jnp.concatenate([jnp.where(tri, 1.0, 0.0), jnp.where(same, 1.0, 0.0)], axis=0).astype(BF16)
    w_up = wup_ref[...].astype(BF16)
    scale = GLA_DK ** -0.5

    def decayed_operands(si, slot):
        rs = slice(si * sub, (si + 1) * sub)
        x = _dot(r_ref[rs, :].astype(BF16), w_up) + b_ref[...]
        la = (jnp.minimum(x, 0.0) - jnp.log(1.0 + jnp.exp(-jnp.abs(x)))) * (1.0 / GLA_TAU)
        sums = _dot(sum_mat, la.astype(BF16))
        cum, tot = sums[:sub], sums[sub:]
        e_tot = jnp.exp(tot)
        et_ref[slot] = e_tot
        qd_ref[slot] = (q_ref[rs, :].astype(F32) * (scale * jnp.exp(cum))).astype(BF16)
        k_in = k_ref[rs, :].astype(F32) * jnp.exp(-cum)
        kst_ref[slot] = (k_in * e_tot).astype(BF16)
        kin_ref[slot] = k_in.astype(BF16)

    def heads(si, slot):
        rs = slice(si * sub, (si + 1) * sub)
        for h in range(GLA_HEADS):
            ks = slice(h * GLA_DK, (h + 1) * GLA_DK)
            vs = slice(h * GLA_DV, (h + 1) * GLA_DV)
            v = v_ref[rs, vs]
            qd = qd_ref[slot, :, ks]
            sc = lax.dot_general(qd, kin_ref[slot, :, ks], NT_DIMS, preferred_element_type=F32)
            o = _dot(jnp.where(tri, sc, 0.0).astype(BF16), v)
            st = st_ref[h]
            inter = [None] * nch
            for ci in (range(nch) if forward else range(nch - 1, -1, -1)):
                cr = slice(ci * c, (ci + 1) * c)
                inter[ci] = lax.dot_general(qd[cr, :], st.astype(BF16), NT_DIMS, preferred_element_type=F32)
                upd = lax.dot_general(v[cr, :], kst_ref[slot, cr, ks], TN_DIMS, preferred_element_type=F32)
                st = st * et_ref[slot, ci * c:ci * c + 1, ks] + upd
            st_ref[h] = st
            o = o + jnp.concatenate(inter, axis=0)
            seq_rows = pl.ds(pl.multiple_of(seq_row0 + si * sub, sub), sub)
            if forward:
                of_ref[seq_rows, vs] = o
            else:
                y = _rms(of_ref[seq_rows, vs] + o, g_ref[...])
                gg = gg_ref[rs, vs].astype(F32)
                o_ref[rs, vs] = (y * (gg * jax.nn.sigmoid(gg))).astype(o_ref.dtype)

    order = list(range(nsub) if forward else range(nsub - 1, -1, -1))
    decayed_operands(order[0], 0)
    for n, si in enumerate(order):
        if n + 1 < nsub:
            decayed_operands(order[n + 1], (n + 1) % 2)
        heads(si, n % 2)


def _gla_kernel(q_ref, k_ref, v_ref, gg_ref, r_ref, wup_ref, b_ref, g_ref,
                o_ref, of_ref, st_ref, *stage_refs, rows, nblk, sub):
    ph = pl.program_id(1)
    i = pl.program_id(2)

    @pl.when(i == 0)
    def _():
        st_ref[...] = jnp.zeros_like(st_ref)

    refs = (q_ref, k_ref, v_ref, gg_ref, r_ref, wup_ref, b_ref, g_ref, o_ref, of_ref, st_ref) + stage_refs

    @pl.when(ph == 0)
    def _():
        _gla_block(1, *refs, rows=rows, sub=sub, seq_row0=i * rows)

    @pl.when(ph == 1)
    def _():
        _gla_block(-1, *refs, rows=rows, sub=sub, seq_row0=(nblk - 1 - i) * rows)


def _gla(layer, proj, rank, wup_pad, b_alpha, g_out, *, batch, seq, rows=1024, sub=128):
    nblk = seq // rows
    t = batch * seq

    def row_block(b, ph, i):
        return b * nblk + jnp.where(ph == 0, i, nblk - 1 - i)

    def parked_block(b, ph, i):
        return b * nblk + jnp.where(ph == 0, nblk - 1, nblk - 1 - i)

    kern = functools.partial(_gla_kernel, rows=rows, nblk=nblk, sub=sub)
    return pl.pallas_call(
        kern,
        grid=(batch, 2, nblk),
        in_specs=[
            pl.BlockSpec((rows, GLA_KEY), lambda b, ph, i: (row_block(b, ph, i), OFF_GQ // GLA_KEY)),
            pl.BlockSpec((rows, GLA_KEY), lambda b, ph, i: (row_block(b, ph, i), OFF_GK // GLA_KEY)),
            pl.BlockSpec((rows, GLA_VAL), lambda b, ph, i: (row_block(b, ph, i), OFF_GV // GLA_VAL)),
            pl.BlockSpec((rows, GLA_VAL), lambda b, ph, i: (parked_block(b, ph, i), OFF_GG // GLA_VAL)),
            pl.BlockSpec((rows, RANK_PAD), lambda b, ph, i: (row_block(b, ph, i), 0)),
            pl.BlockSpec((None, None, RANK_PAD, GLA_KEY), lambda b, ph, i: (layer, ph, 0, 0)),
            pl.BlockSpec((None, None, 1, GLA_KEY), lambda b, ph, i: (layer, ph, 0, 0)),
            _layer_resident(layer, (1, GLA_DV)),
        ],
        out_specs=pl.BlockSpec((rows, GLA_VAL), lambda b, ph, i: (parked_block(b, ph, i), 0)),
        out_shape=jax.ShapeDtypeStruct((t, GLA_VAL), BF16),
        scratch_shapes=[
            pltpu.VMEM((seq, GLA_VAL), F32),
            pltpu.VMEM((GLA_HEADS, GLA_DV, GLA_DK), F32),
            pltpu.VMEM((2, sub, GLA_KEY), BF16),
            pltpu.VMEM((2, sub, GLA_KEY), BF16),
            pltpu.VMEM((2, sub, GLA_KEY), BF16),
            pltpu.VMEM((2, sub, GLA_KEY), F32),
        ],
        compiler_params=pltpu.CompilerParams(
            dimension_semantics=("parallel", "arbitrary", "arbitrary"),
            vmem_limit_bytes=VMEM_LIMIT_BYTES,
        ),
        name="gla",
    )(proj, proj, proj, proj, rank, wup_pad, b_alpha, g_out)


def _attn_kernel(q_ref, k_ref, v_ref, cos_ref, sa_ref, sb_ref, gq_ref, gk_ref,
                 o_ref, kn_ref, v1_ref, m_ref, *s_refs, tq, rt, kc):
    qi = pl.program_id(2)
    nq = pl.num_programs(2)
    seq = kn_ref.shape[0]
    tiles = [(r, hh) for r in range(tq // rt) for hh in range(ATTN_GROUP)]
    assert len(s_refs) == 2 and len(tiles) % 2 == 0
    carried_s_ref = s_refs[(len(tiles) - 1) % 2]

    def rope(x, rows):
        return (x * cos_ref[rows, :]
                + pltpu.roll(x, HEAD_DIM - 32, 1) * sa_ref[rows, :]
                + pltpu.roll(x, 32, 1) * sb_ref[rows, :])

    @pl.when(qi == 0)
    def _():
        k = _rms(k_ref[...].astype(F32), gk_ref[...])
        kn_ref[...] = rope(k, slice(None)).astype(BF16)
        v1_ref[:, :HEAD_DIM] = v_ref[...]
        v1_ref[:, HEAD_DIM:] = jnp.ones((seq, HEAD_DIM), BF16)
        carried_s_ref[...] = jnp.zeros_like(carried_s_ref)
        m_ref[...] = jnp.zeros_like(m_ref)

    scale = HEAD_DIM ** -0.5 * LOG2_E

    def pass1(r, hh, s_ref):
        x = _rms(q_ref[r * rt:(r + 1) * rt, hh * HEAD_DIM:(hh + 1) * HEAD_DIM].astype(F32), gq_ref[...])
        rows = pl.ds(pl.multiple_of(qi * tq + r * rt, rt), rt)
        q = (rope(x, rows) * scale).astype(BF16)
        m = jnp.full((rt, HEAD_DIM), -jnp.inf, F32)
        for c0 in range(0, seq, kc):
            s = lax.dot_general(q, kn_ref[c0:c0 + kc, :], NT_DIMS, preferred_element_type=F32)
            s_ref[:, c0:c0 + kc] = s
            for l0 in range(0, kc, HEAD_DIM):
                m = jnp.maximum(m, s[:, l0:l0 + HEAD_DIM])
        return jnp.max(m, axis=-1, keepdims=True)

    def pass2(row0, hh, s_ref, m):
        acc = jnp.zeros((rt, 2 * HEAD_DIM), F32)
        for c0 in range(0, seq, kc):
            p = jnp.exp2((s_ref[:, c0:c0 + kc] - m).astype(BF16))
            acc = acc + _dot(p, v1_ref[c0:c0 + kc, :])
        o = acc[:, :HEAD_DIM] / acc[:, HEAD_DIM:HEAD_DIM + 1]
        rows = pl.ds(pl.multiple_of(row0, rt), rt)
        o_ref[rows, hh * HEAD_DIM:(hh + 1) * HEAD_DIM] = o.astype(o_ref.dtype)

    last_r, last_hh = tiles[-1]
    prev_block = jnp.where(qi == 0, nq - 1, qi - 1)
    pending = (prev_block * tq + last_r * rt, last_hh, carried_s_ref, m_ref[:, :1])
    for t, (r, hh) in enumerate(tiles):
        s_ref = s_refs[t % 2]
        m = pass1(r, hh, s_ref)
        pass2(*pending)
        pending = (qi * tq + r * rt, hh, s_ref, m)
    m_ref[...] = jnp.broadcast_to(pending[3], m_ref.shape)

    @pl.when(qi == nq - 1)
    def _():
        pass2(*pending)


def _attn(layer, proj, cos_t, sa_t, sb_t, g_q, g_k, *, batch, seq, tq=1024, rt=128, kc=512):
    nq = seq // tq
    t = batch * seq
    gw = ATTN_GROUP * HEAD_DIM
    kern = functools.partial(_attn_kernel, tq=tq, rt=rt, kc=kc)
    return pl.pallas_call(
        kern,
        grid=(batch, ATTN_KV_HEADS, nq),
        in_specs=[
            pl.BlockSpec((tq, gw), lambda b, g, i: (b * nq + i, OFF_AQ // gw + g)),
            pl.BlockSpec((seq, HEAD_DIM), lambda b, g, i: (b, OFF_AK // HEAD_DIM + g)),
            pl.BlockSpec((seq, HEAD_DIM), lambda b, g, i: (b, OFF_AV // HEAD_DIM + g)),
            _resident((seq, HEAD_DIM)),
            _resident((seq, HEAD_DIM)),
            _resident((seq, HEAD_DIM)),
            _layer_resident(layer, (1, HEAD_DIM)),
            _layer_resident(layer, (1, HEAD_DIM)),
        ],
        out_specs=pl.BlockSpec((seq, gw), lambda b, g, i: (b, g)),
        out_shape=jax.ShapeDtypeStruct((t, ATTN_Q), BF16),
        scratch_shapes=[
            pltpu.VMEM((seq, HEAD_DIM), BF16),
            pltpu.VMEM((seq, 2 * HEAD_DIM), BF16),
            pltpu.VMEM((rt, HEAD_DIM), F32),
            pltpu.VMEM((rt, seq), F32),
            pltpu.VMEM((rt, seq), F32),
        ],
        compiler_params=pltpu.CompilerParams(
            dimension_semantics=("parallel", "parallel", "arbitrary"),
            vmem_limit_bytes=VMEM_LIMIT_BYTES,
        ),
        name="attn",
    )(proj, proj, proj, cos_t, sa_t, sb_t, g_q, g_k)


def _merge_kernel(a_ref, at_ref, ga_ref, gb_ref, h_ref, woa_ref, wob_ref, wout_ref, g_ref, o_ref,
                  *, sub):
    for r0 in range(0, o_ref.shape[0], sub):
        rs = slice(r0, r0 + sub)
        branch_a = _dot(a_ref[rs, :], woa_ref[...])
        branch_b = _dot(at_ref[rs, :], wob_ref[...])
        mixed = (jax.nn.sigmoid(ga_ref[rs, :].astype(F32)) * branch_a
                 + jax.nn.sigmoid(gb_ref[rs, :].astype(F32)) * branch_b)
        y = _dot(mixed.astype(BF16), wout_ref[...])
        o_ref[rs, :] = h_ref[rs, :] + _rms(y, g_ref[...])


def _merge(layer, gla_out, attn_out, proj, h, w_o_gla, w_o_attn, w_out, g_post, *, tm=1024, sub=256):
    t = h.shape[0]
    row = lambda i: (i, 0)
    return pl.pallas_call(
        functools.partial(_merge_kernel, sub=sub),
        grid=(t // tm,),
        in_specs=[
            pl.BlockSpec((tm, GLA_VAL), row),
            pl.BlockSpec((tm, ATTN_Q), row),
            pl.BlockSpec((tm, D_MODEL), lambda i: (i, OFF_GATE_A // D_MODEL)),
            pl.BlockSpec((tm, D_MODEL), lambda i: (i, OFF_GATE_B // D_MODEL)),
            pl.BlockSpec((tm, D_MODEL), row),
            _layer_resident(layer, (GLA_VAL, D_MODEL)),
            _layer_resident(layer, (ATTN_Q, D_MODEL)),
            _layer_resident(layer, (D_MODEL, D_MODEL)),
            _layer_resident(layer, (1, D_MODEL)),
        ],
        out_specs=pl.BlockSpec((tm, D_MODEL), row),
        out_shape=jax.ShapeDtypeStruct((t, D_MODEL), F32),
        compiler_params=pltpu.CompilerParams(
            dimension_semantics=("parallel",),
            vmem_limit_bytes=VMEM_LIMIT_BYTES,
        ),
        name="merge",
    )(gla_out, attn_out, proj, proj, h, w_o_gla, w_o_attn, w_out, g_post)


def _ffn_kernel(h_ref, p_ref, gpre_ref, win_ref, wout_ref, gpost_ref, wpp_ref, wpg_ref, gple_ref,
                o_ref, *, hidden_chunk, sub):
    for r0 in range(0, o_ref.shape[0], sub):
        rs = slice(r0, r0 + sub)
        h = h_ref[rs, :]
        x = _rms(h, gpre_ref[...]).astype(BF16)
        acc = jnp.zeros(h.shape, F32)
        for c0 in range(0, FFN_HIDDEN, hidden_chunk):
            gate = _dot(x, win_ref[:, c0:c0 + hidden_chunk])
            up = _dot(x, win_ref[:, FFN_HIDDEN + c0:FFN_HIDDEN + c0 + hidden_chunk])
            act = (gate * jax.nn.sigmoid(gate)) * up
            acc = acc + _dot(act.astype(BF16), wout_ref[c0:c0 + hidden_chunk, :])
        h = h + _rms(acc, gpost_ref[...])
        e = _dot(p_ref[rs, :].astype(BF16), wpp_ref[...])
        gate = jax.nn.sigmoid(_dot(h.astype(BF16), wpg_ref[...]))
        o_ref[rs, :] = h + _rms(gate * e, gple_ref[...])


def _ffn(layer, h, p, g_pre, w_in, w_out, g_post, w_pp, w_pg, g_ple, *, seq, tm=1024, sub=256,
         hidden_chunk=FFN_HIDDEN):
    t = h.shape[0]
    row = lambda i: (i, 0)
    tiles_per_seq = seq // tm
    kern = functools.partial(_ffn_kernel, hidden_chunk=hidden_chunk, sub=sub)
    return pl.pallas_call(
        kern,
        grid=(t // tm,),
        in_specs=[
            pl.BlockSpec((tm, D_MODEL), row),
            pl.BlockSpec((None, None, tm, PLE_DIM),
                         lambda i: (layer, i // tiles_per_seq, i % tiles_per_seq, 0)),
            _layer_resident(layer, (1, D_MODEL)),
            _layer_resident(layer, (D_MODEL, 2 * FFN_HIDDEN)),
            _layer_resident(layer, (FFN_HIDDEN, D_MODEL)),
            _layer_resident(layer, (1, D_MODEL)),
            _layer_resident(layer, (PLE_DIM, D_MODEL)),
            _layer_resident(layer, (D_MODEL, D_MODEL)),
            _layer_resident(layer, (1, D_MODEL)),
        ],
        out_specs=pl.BlockSpec((tm, D_MODEL), row),
        out_shape=jax.ShapeDtypeStruct((t, D_MODEL), F32),
        compiler_params=pltpu.CompilerParams(
            dimension_semantics=("parallel",),
            vmem_limit_bytes=VMEM_LIMIT_BYTES,
        ),
        name="ffn",
    )(h, p, g_pre, w_in, w_out, g_post, w_pp, w_pg, g_ple)


def _rope_tables(seq):
    rows = seq // GRID_W
    row = jnp.repeat(jnp.arange(rows, dtype=F32), GRID_W)
    col = jnp.tile(jnp.arange(GRID_W, dtype=F32), rows)
    inv = ROPE_THETA ** (-jnp.arange(0, ROPE_AXIS_DIM, 2, dtype=F32) / ROPE_AXIS_DIM)
    ang = jnp.stack([row[:, None] * inv, col[:, None] * inv], axis=1)
    cos, sin = jnp.cos(ang), jnp.sin(ang)
    zero = jnp.zeros_like(sin)
    lanes = lambda a, b: jnp.stack([a, b], axis=2).reshape(seq, HEAD_DIM)
    return lanes(cos, cos), lanes(-sin, zero), lanes(zero, sin)


def _permute_w_in(w_in):
    sizes = (GLA_KEY, GLA_KEY, GLA_VAL, GLA_VAL, GLA_RANK, GLA_RANK,
             ATTN_Q, ATTN_KV, ATTN_KV, D_MODEL, D_MODEL)
    offs = np.cumsum((0,) + sizes)
    gq, gk, gv, gg, ra_f, ra_b, aq, ak, av, gate_a, gate_b = (
        w_in[:, :, offs[n]:offs[n + 1]] for n in range(len(sizes)))
    main = jnp.concatenate([gv, gg, aq, gate_a, gate_b, gq, gk, ak, av], axis=2)
    assert main.shape[2] == PROJ_WIDTH
    rank = jnp.concatenate([ra_f, ra_b], axis=2)
    rank = jnp.pad(rank, ((0, 0), (0, 0), (0, RANK_PAD - 2 * GLA_RANK)))
    return main.astype(BF16), rank.astype(BF16)


def _pad_w_alpha(w_alpha_up):
    pads = [((0, 0), (d * GLA_RANK, RANK_PAD - (d + 1) * GLA_RANK), (0, 0)) for d in range(2)]
    return jnp.stack([jnp.pad(w_alpha_up[:, d], pads[d]) for d in range(2)], axis=1)


def kernel(x, p, g_mix_pre, w_in, w_alpha_up, b_alpha, g_gla_out, g_q_norm, g_k_norm, w_o_gla,
           w_o_attn, w_out, g_mix_post, g_ffn_pre, w_ffn_in, w_ffn_out, g_ffn_post, w_ple_proj,
           w_ple_gate, g_ple_post):
    batch, seq, _ = x.shape
    depth = p.shape[0]
    t = batch * seq
    cos_t, sa_t, sb_t = _rope_tables(seq)
    gain = lambda g: g.reshape(depth, 1, -1)
    w_main, w_rank = _permute_w_in(w_in)
    wup_pad = _pad_w_alpha(w_alpha_up)
    b_alpha = b_alpha.reshape(depth, 2, 1, GLA_KEY)
    w_o_gla, w_o_attn, w_out, w_ffn_in, w_ffn_out, w_ple_proj, w_ple_gate = (
        w.astype(BF16) for w in (w_o_gla, w_o_attn, w_out, w_ffn_in, w_ffn_out, w_ple_proj, w_ple_gate))
    h = x.reshape(t, D_MODEL)
    for i in range(depth):
        proj, rank = _proj(i, h, gain(g_mix_pre), w_main, w_rank)
        gla_out = _gla(i, proj, rank, wup_pad, b_alpha, gain(g_gla_out), batch=batch, seq=seq)
        attn_out = _attn(i, proj, cos_t, sa_t, sb_t, gain(g_q_norm), gain(g_k_norm),
                         batch=batch, seq=seq)
        h = _merge(i, gla_out, attn_out, proj, h, w_o_gla, w_o_attn, w_out, gain(g_mix_post))
        h = _ffn(i, h, p, gain(g_ffn_pre), w_ffn_in, w_ffn_out, gain(g_ffn_post), w_ple_proj,
                 w_ple_gate, gain(g_ple_post), seq=seq)
    return h.reshape(batch, seq, D_MODEL)
```

```python
import functools

import jax
import jax.numpy as jnp
import numpy as np
from jax import lax
from jax.experimental import pallas as pl
from jax.experimental.pallas import tpu as pltpu

F32 = jnp.float32
BF16 = jnp.bfloat16

D_MODEL = 1024
GRID_W = 64
PLE_DIM = 256
EPS = 1e-6
GLA_HEADS = 4
GLA_DK = 128
GLA_DV = 256
GLA_KEY = GLA_HEADS * GLA_DK
GLA_VAL = GLA_HEADS * GLA_DV
GLA_RANK = 16
GLA_TAU = 16.0
GLA_CHUNK = 64
GLA_CHUNK_SHIFT = 6
assert 1 << GLA_CHUNK_SHIFT == GLA_CHUNK
ATTN_HEADS = 8
ATTN_KV_HEADS = 2
ATTN_GROUP = ATTN_HEADS // ATTN_KV_HEADS
HEAD_DIM = 128
ATTN_Q = ATTN_HEADS * HEAD_DIM
ATTN_KV = ATTN_KV_HEADS * HEAD_DIM
ROPE_THETA = 10000.0
ROPE_AXIS_DIM = HEAD_DIM // 2
FFN_HIDDEN = 2816
LOG2_E = 1.4426950408889634

OFF_GV = 0
OFF_GG = 1024
OFF_AQ = 2048
OFF_GATE_A = 3072
OFF_GATE_B = 4096
OFF_GQ = 5120
OFF_GK = 5632
OFF_AK = 6144
OFF_AV = 6400
PROJ_WIDTH = 6656
RANK_PAD = 128

VMEM_LIMIT_BYTES = 56 * 1024 * 1024

NT_DIMS = (((1,), (1,)), ((), ()))
TN_DIMS = (((0,), (0,)), ((), ()))


def _rms(x, g):
    return x * lax.rsqrt(jnp.mean(x * x, axis=-1, keepdims=True) + EPS) * g


def _dot(a, b):
    return jnp.dot(a, b, preferred_element_type=F32)


def _resident(shape):
    zeros = (0,) * len(shape)
    return pl.BlockSpec(shape, lambda *_: zeros, pipeline_mode=pl.Buffered(1))


def _layer_resident(layer, shape):
    index = (layer,) + (0,) * len(shape)
    return pl.BlockSpec((None,) + shape, lambda *_: index, pipeline_mode=pl.Buffered(1))


def _proj_kernel(x_ref, g_ref, w_ref, wr_ref, cos_ref, sa_ref, sb_ref, gk_ref, o_ref, r_ref, *, tn, sub):
    for r0 in range(0, o_ref.shape[0], sub):
        rs = slice(r0, r0 + sub)
        u = _rms(x_ref[rs, :], g_ref[...]).astype(BF16)
        r_ref[rs, :] = _dot(u, wr_ref[...])
        for n0 in range(0, w_ref.shape[1], tn):
            y = _dot(u, w_ref[:, n0:n0 + tn])
            if n0 <= OFF_AK < n0 + tn:
                heads = []
                for c0 in range(0, tn, HEAD_DIM):
                    head = y[:, c0:c0 + HEAD_DIM]
                    if OFF_AK <= n0 + c0 < OFF_AK + ATTN_KV:
                        head = _rms(head, gk_ref[...])
                        head = (head * cos_ref[rs, :]
                                + pltpu.roll(head, HEAD_DIM - 32, 1) * sa_ref[rs, :]
                                + pltpu.roll(head, 32, 1) * sb_ref[rs, :])
                    heads.append(head)
                y = jnp.concatenate(heads, axis=1)
            o_ref[rs, n0:n0 + tn] = y.astype(o_ref.dtype)


def _proj(layer, h, g, w, w_rank, cos_t, sa_t, sb_t, g_k, *, seq, tm=512, sub=256, tn=512):
    t = h.shape[0]
    n = w.shape[2]
    row = lambda i: (i, 0)
    pos = lambda i: (i % (seq // tm), 0)
    return pl.pallas_call(
        functools.partial(_proj_kernel, tn=tn, sub=sub),
        grid=(t // tm,),
        in_specs=[
            pl.BlockSpec((tm, D_MODEL), row),
            _layer_resident(layer, (1, D_MODEL)),
            _layer_resident(layer, (D_MODEL, n)),
            _layer_resident(layer, (D_MODEL, RANK_PAD)),
            pl.BlockSpec((tm, HEAD_DIM), pos),
            pl.BlockSpec((tm, HEAD_DIM), pos),
            pl.BlockSpec((tm, HEAD_DIM), pos),
            _layer_resident(layer, (1, HEAD_DIM)),
        ],
        out_specs=[
            pl.BlockSpec((tm, n), row),
            pl.BlockSpec((tm, RANK_PAD), row),
        ],
        out_shape=[
            jax.ShapeDtypeStruct((t, n), BF16),
            jax.ShapeDtypeStruct((t, RANK_PAD), F32),
        ],
        compiler_params=pltpu.CompilerParams(
            dimension_semantics=("parallel",),
            vmem_limit_bytes=VMEM_LIMIT_BYTES,
        ),
        name="proj",
    )(h, g, w, w_rank, cos_t, sa_t, sb_t, g_k)


def _gla_block(direction, q_ref, k_ref, v_ref, gg_ref, r_ref, wup_ref, b_ref, g_ref,
               o_ref, of_ref, st_ref, qd_ref, kin_ref, kst_ref, et_ref, *, rows, sub, seq_row0):
    c = GLA_CHUNK
    nsub = rows // sub
    nch = sub // c
    forward = direction > 0
    row = lax.broadcasted_iota(jnp.int32, (sub, sub), 0)
    col = lax.broadcasted_iota(jnp.int32, (sub, sub), 1)
    same = (row >> GLA_CHUNK_SHIFT) == (col >> GLA_CHUNK_SHIFT)
    tri = jnp.logical_and(same, (col <= row) if forward else (col >= row))
    sum_mat = jnp.concatenate([jnp.where(tri, 1.0, 0.0), jnp.where(same, 1.0, 0.0)], axis=0).astype(BF16)
    w_up = wup_ref[...].astype(BF16)
    scale = GLA_DK ** -0.5

    def decayed_operands(si, slot):
        rs = slice(si * sub, (si + 1) * sub)
        x = _dot(r_ref[rs, :].astype(BF16), w_up) + b_ref[...]
        la = (jnp.minimum(x, 0.0) - jnp.log(1.0 + jnp.exp(-jnp.abs(x)))) * (1.0 / GLA_TAU)
        sums = _dot(sum_mat, la.astype(BF16))
        cum, tot = sums[:sub], sums[sub:]
        e_tot = jnp.exp(tot)
        et_ref[slot] = e_tot
        qd_ref[slot] = (q_ref[rs, :].astype(F32) * (scale * jnp.exp(cum))).astype(BF16)
        k_in = k_ref[rs, :].astype(F32) * jnp.exp(-cum)
        kst_ref[slot] = (k_in * e_tot).astype(BF16)
        kin_ref[slot] = k_in.astype(BF16)

    def heads(si, slot):
        rs = slice(si * sub, (si + 1) * sub)
        for h in range(GLA_HEADS):
            ks = slice(h * GLA_DK, (h + 1) * GLA_DK)
            vs = slice(h * GLA_DV, (h + 1) * GLA_DV)
            v = v_ref[rs, vs]
            qd = qd_ref[slot, :, ks]
            sc = lax.dot_general(qd, kin_ref[slot, :, ks], NT_DIMS, preferred_element_type=F32)
            o = _dot(jnp.where(tri, sc, 0.0).astype(BF16), v)
            st = st_ref[h]
            inter = [None] * nch
            for ci in (range(nch) if forward else range(nch - 1, -1, -1)):
                cr = slice(ci * c, (ci + 1) * c)
                inter[ci] = lax.dot_general(qd[cr, :], st.astype(BF16), NT_DIMS, preferred_element_type=F32)
                upd = lax.dot_general(v[cr, :], kst_ref[slot, cr, ks], TN_DIMS, preferred_element_type=F32)
                st = st * et_ref[slot, ci * c:ci * c + 1, ks] + upd
            st_ref[h] = st
            o = o + jnp.concatenate(inter, axis=0)
            seq_rows = pl.ds(pl.multiple_of(seq_row0 + si * sub, sub), sub)
            if forward:
                of_ref[seq_rows, vs] = o
            else:
                y = _rms(of_ref[seq_rows, vs] + o, g_ref[...])
                gg = gg_ref[rs, vs].astype(F32)
                o_ref[rs, vs] = (y * (gg * jax.nn.sigmoid(gg))).astype(o_ref.dtype)

    order = list(range(nsub) if forward else range(nsub - 1, -1, -1))
    decayed_operands(order[0], 0)
    for n, si in enumerate(order):
        if n + 1 < nsub:
            decayed_operands(order[n + 1], (n + 1) % 2)
        heads(si, n % 2)


def _gla_kernel(q_ref, k_ref, v_ref, gg_ref, r_ref, wup_ref, b_ref, g_ref,
                o_ref, of_ref, st_ref, *stage_refs, rows, nblk, sub):
    ph = pl.program_id(1)
    i = pl.program_id(2)

    @pl.when(i == 0)
    def _():
        st_ref[...] = jnp.zeros_like(st_ref)

    refs = (q_ref, k_ref, v_ref, gg_ref, r_ref, wup_ref, b_ref, g_ref, o_ref, of_ref, st_ref) + stage_refs

    @pl.when(ph == 0)
    def _():
        _gla_block(1, *refs, rows=rows, sub=sub, seq_row0=i * rows)

    @pl.when(ph == 1)
    def _():
        _gla_block(-1, *refs, rows=rows, sub=sub, seq_row0=(nblk - 1 - i) * rows)


def _gla(layer, proj, rank, wup_pad, b_alpha, g_out, *, batch, seq, rows=1024, sub=128):
    nblk = seq // rows
    t = batch * seq

    def row_block(b, ph, i):
        return b * nblk + jnp.where(ph == 0, i, nblk - 1 - i)

    def parked_block(b, ph, i):
        return b * nblk + jnp.where(ph == 0, nblk - 1, nblk - 1 - i)

    kern = functools.partial(_gla_kernel, rows=rows, nblk=nblk, sub=sub)
    return pl.pallas_call(
        kern,
        grid=(batch, 2, nblk),
        in_specs=[
            pl.BlockSpec((rows, GLA_KEY), lambda b, ph, i: (row_block(b, ph, i), OFF_GQ // GLA_KEY)),
            pl.BlockSpec((rows, GLA_KEY), lambda b, ph, i: (row_block(b, ph, i), OFF_GK // GLA_KEY)),
            pl.BlockSpec((rows, GLA_VAL), lambda b, ph, i: (row_block(b, ph, i), OFF_GV // GLA_VAL)),
            pl.BlockSpec((rows, GLA_VAL), lambda b, ph, i: (parked_block(b, ph, i), OFF_GG // GLA_VAL)),
            pl.BlockSpec((rows, RANK_PAD), lambda b, ph, i: (row_block(b, ph, i), 0)),
            pl.BlockSpec((None, None, RANK_PAD, GLA_KEY), lambda b, ph, i: (layer, ph, 0, 0)),
            pl.BlockSpec((None, None, 1, GLA_KEY), lambda b, ph, i: (layer, ph, 0, 0)),
            _layer_resident(layer, (1, GLA_DV)),
        ],
        out_specs=pl.BlockSpec((rows, GLA_VAL), lambda b, ph, i: (parked_block(b, ph, i), 0)),
        out_shape=jax.ShapeDtypeStruct((t, GLA_VAL), BF16),
        scratch_shapes=[
            pltpu.VMEM((seq, GLA_VAL), F32),
            pltpu.VMEM((GLA_HEADS, GLA_DV, GLA_DK), F32),
            pltpu.VMEM((2, sub, GLA_KEY), BF16),
            pltpu.VMEM((2, sub, GLA_KEY), BF16),
            pltpu.VMEM((2, sub, GLA_KEY), BF16),
            pltpu.VMEM((2, sub, GLA_KEY), F32),
        ],
        compiler_params=pltpu.CompilerParams(
            dimension_semantics=("parallel", "arbitrary", "arbitrary"),
            vmem_limit_bytes=VMEM_LIMIT_BYTES,
        ),
        name="gla",
    )(proj, proj, proj, proj, rank, wup_pad, b_alpha, g_out)


def _attn_kernel(q_ref, k_ref, v_ref, cos_ref, sa_ref, sb_ref, gq_ref, gk_ref,
                 o_ref, kn_ref, v1_ref, m_ref, *s_refs, tq, rt, kc):
    qi = pl.program_id(2)
    nq = pl.num_programs(2)
    seq = kn_ref.shape[0]
    tiles = [(r, hh) for r in range(tq // rt) for hh in range(ATTN_GROUP)]
    assert len(s_refs) == 2 and len(tiles) % 2 == 0
    carried_s_ref = s_refs[(len(tiles) - 1) % 2]

    def rope(x, rows):
        return (x * cos_ref[rows, :]
                + pltpu.roll(x, HEAD_DIM - 32, 1) * sa_ref[rows, :]
                + pltpu.roll(x, 32, 1) * sb_ref[rows, :])

    @pl.when(qi == 0)
    def _():
        kn_ref[...] = k_ref[...]
        v1_ref[:, :HEAD_DIM] = v_ref[...]
        v1_ref[:, HEAD_DIM:] = jnp.ones((seq, HEAD_DIM), BF16)
        carried_s_ref[...] = jnp.zeros_like(carried_s_ref)
        m_ref[...] = jnp.zeros_like(m_ref)

    scale = HEAD_DIM ** -0.5 * LOG2_E

    def pass1(r, hh, s_ref):
        x = _rms(q_ref[r * rt:(r + 1) * rt, hh * HEAD_DIM:(hh + 1) * HEAD_DIM].astype(F32), gq_ref[...])
        rows = pl.ds(pl.multiple_of(qi * tq + r * rt, rt), rt)
        q = (rope(x, rows) * scale).astype(BF16)
        m = jnp.full((rt, HEAD_DIM), -jnp.inf, F32)
        for c0 in range(0, seq, kc):
            s = lax.dot_general(q, kn_ref[c0:c0 + kc, :], NT_DIMS, preferred_element_type=F32)
            s_ref[:, c0:c0 + kc] = s
            for l0 in range(0, kc, HEAD_DIM):
                m = jnp.maximum(m, s[:, l0:l0 + HEAD_DIM])
        return jnp.max(m, axis=-1, keepdims=True)

    def pass2(row0, hh, s_ref, m):
        acc = jnp.zeros((rt, 2 * HEAD_DIM), F32)
        for c0 in range(0, seq, kc):
            p = jnp.exp2((s_ref[:, c0:c0 + kc] - m).astype(BF16))
            acc = acc + _dot(p, v1_ref[c0:c0 + kc, :])
        o = acc[:, :HEAD_DIM] / acc[:, HEAD_DIM:HEAD_DIM + 1]
        rows = pl.ds(pl.multiple_of(row0, rt), rt)
        o_ref[rows, hh * HEAD_DIM:(hh + 1) * HEAD_DIM] = o.astype(o_ref.dtype)

    last_r, last_hh = tiles[-1]
    prev_block = jnp.where(qi == 0, nq - 1, qi - 1)
    pending = (prev_block * tq + last_r * rt, last_hh, carried_s_ref, m_ref[:, :1])
    for t, (r, hh) in enumerate(tiles):
        s_ref = s_refs[t % 2]
        m = pass1(r, hh, s_ref)
        pass2(*pending)
        pending = (qi * tq + r * rt, hh, s_ref, m)
    m_ref[...] = jnp.broadcast_to(pending[3], m_ref.shape)

    @pl.when(qi == nq - 1)
    def _():
        pass2(*pending)


def _attn(layer, proj, cos_t, sa_t, sb_t, g_q, g_k, *, batch, seq, tq=1024, rt=128, kc=512):
    nq = seq // tq
    t = batch * seq
    gw = ATTN_GROUP * HEAD_DIM
    kern = functools.partial(_attn_kernel, tq=tq, rt=rt, kc=kc)
    return pl.pallas_call(
        kern,
        grid=(batch, ATTN_KV_HEADS, nq),
        in_specs=[
            pl.BlockSpec((tq, gw), lambda b, g, i: (b * nq + i, OFF_AQ // gw + g)),
            pl.BlockSpec((seq, HEAD_DIM), lambda b, g, i: (b, OFF_AK // HEAD_DIM + g)),
            pl.BlockSpec((seq, HEAD_DIM), lambda b, g, i: (b, OFF_AV // HEAD_DIM + g)),
            _resident((seq, HEAD_DIM)),
            _resident((seq, HEAD_DIM)),
            _resident((seq, HEAD_DIM)),
            _layer_resident(layer, (1, HEAD_DIM)),
            _layer_resident(layer, (1, HEAD_DIM)),
        ],
        out_specs=pl.BlockSpec((seq, gw), lambda b, g, i: (b, g)),
        out_shape=jax.ShapeDtypeStruct((t, ATTN_Q), BF16),
        scratch_shapes=[
            pltpu.VMEM((seq, HEAD_DIM), BF16),
            pltpu.VMEM((seq, 2 * HEAD_DIM), BF16),
            pltpu.VMEM((rt, HEAD_DIM), F32),
            pltpu.VMEM((rt, seq), F32),
            pltpu.VMEM((rt, seq), F32),
        ],
        compiler_params=pltpu.CompilerParams(
            dimension_semantics=("parallel", "parallel", "arbitrary"),
            vmem_limit_bytes=VMEM_LIMIT_BYTES,
        ),
        name="attn",
    )(proj, proj, proj, cos_t, sa_t, sb_t, g_q, g_k)


def _merge_kernel(a_ref, at_ref, ga_ref, gb_ref, h_ref, woa_ref, wob_ref, wout_ref, g_ref, o_ref,
                  *, sub):
    for r0 in range(0, o_ref.shape[0], sub):
        rs = slice(r0, r0 + sub)
        branch_a = _dot(a_ref[rs, :], woa_ref[...])
        branch_b = _dot(at_ref[rs, :], wob_ref[...])
        mixed = (jax.nn.sigmoid(ga_ref[rs, :].astype(F32)) * branch_a
                 + jax.nn.sigmoid(gb_ref[rs, :].astype(F32)) * branch_b)
        y = _dot(mixed.astype(BF16), wout_ref[...])
        o_ref[rs, :] = h_ref[rs, :] + _rms(y, g_ref[...])


def _merge(layer, gla_out, attn_out, proj, h, w_o_gla, w_o_attn, w_out, g_post, *, tm=1024, sub=256):
    t = h.shape[0]
    row = lambda i: (i, 0)
    return pl.pallas_call(
        functools.partial(_merge_kernel, sub=sub),
        grid=(t // tm,),
        in_specs=[
            pl.BlockSpec((tm, GLA_VAL), row),
            pl.BlockSpec((tm, ATTN_Q), row),
            pl.BlockSpec((tm, D_MODEL), lambda i: (i, OFF_GATE_A // D_MODEL)),
            pl.BlockSpec((tm, D_MODEL), lambda i: (i, OFF_GATE_B // D_MODEL)),
            pl.BlockSpec((tm, D_MODEL), row),
            _layer_resident(layer, (GLA_VAL, D_MODEL)),
            _layer_resident(layer, (ATTN_Q, D_MODEL)),
            _layer_resident(layer, (D_MODEL, D_MODEL)),
            _layer_resident(layer, (1, D_MODEL)),
        ],
        out_specs=pl.BlockSpec((tm, D_MODEL), row),
        out_shape=jax.ShapeDtypeStruct((t, D_MODEL), F32),
        compiler_params=pltpu.CompilerParams(
            dimension_semantics=("parallel",),
            vmem_limit_bytes=VMEM_LIMIT_BYTES,
        ),
        name="merge",
    )(gla_out, attn_out, proj, proj, h, w_o_gla, w_o_attn, w_out, g_post)


def _ffn_kernel(h_ref, p_ref, gpre_ref, win_ref, wout_ref, gpost_ref, wpp_ref, wpg_ref, gple_ref,
                o_ref, *, hidden_chunk, sub):
    for r0 in range(0, o_ref.shape[0], sub):
        rs = slice(r0, r0 + sub)
        h = h_ref[rs, :]
        x = _rms(h, gpre_ref[...]).astype(BF16)
        acc = jnp.zeros(h.shape, F32)
        for c0 in range(0, FFN_HIDDEN, hidden_chunk):
            gate = _dot(x, win_ref[:, c0:c0 + hidden_chunk])
            up = _dot(x, win_ref[:, FFN_HIDDEN + c0:FFN_HIDDEN + c0 + hidden_chunk])
            act = (gate * jax.nn.sigmoid(gate)) * up
            acc = acc + _dot(act.astype(BF16), wout_ref[c0:c0 + hidden_chunk, :])
        h = h + _rms(acc, gpost_ref[...])
        e = _dot(p_ref[rs, :].astype(BF16), wpp_ref[...])
        gate = jax.nn.sigmoid(_dot(h.astype(BF16), wpg_ref[...]))
        o_ref[rs, :] = h + _rms(gate * e, gple_ref[...])


def _ffn(layer, h, p, g_pre, w_in, w_out, g_post, w_pp, w_pg, g_ple, *, seq, tm=1024, sub=256,
         hidden_chunk=FFN_HIDDEN):
    t = h.shape[0]
    row = lambda i: (i, 0)
    tiles_per_seq = seq // tm
    kern = functools.partial(_ffn_kernel, hidden_chunk=hidden_chunk, sub=sub)
    return pl.pallas_call(
        kern,
        grid=(t // tm,),
        in_specs=[
            pl.BlockSpec((tm, D_MODEL), row),
            pl.BlockSpec((None, None, tm, PLE_DIM),
                         lambda i: (layer, i // tiles_per_seq, i % tiles_per_seq, 0)),
            _layer_resident(layer, (1, D_MODEL)),
            _layer_resident(layer, (D_MODEL, 2 * FFN_HIDDEN)),
            _layer_resident(layer, (FFN_HIDDEN, D_MODEL)),
            _layer_resident(layer, (1, D_MODEL)),
            _layer_resident(layer, (PLE_DIM, D_MODEL)),
            _layer_resident(layer, (D_MODEL, D_MODEL)),
            _layer_resident(layer, (1, D_MODEL)),
        ],
        out_specs=pl.BlockSpec((tm, D_MODEL), row),
        out_shape=jax.ShapeDtypeStruct((t, D_MODEL), F32),
        compiler_params=pltpu.CompilerParams(
            dimension_semantics=("parallel",),
            vmem_limit_bytes=VMEM_LIMIT_BYTES,
        ),
        name="ffn",
    )(h, p, g_pre, w_in, w_out, g_post, w_pp, w_pg, g_ple)


def _rope_tables(seq):
    rows = seq // GRID_W
    row = jnp.repeat(jnp.arange(rows, dtype=F32), GRID_W)
    col = jnp.tile(jnp.arange(GRID_W, dtype=F32), rows)
    inv = ROPE_THETA ** (-jnp.arange(0, ROPE_AXIS_DIM, 2, dtype=F32) / ROPE_AXIS_DIM)
    ang = jnp.stack([row[:, None] * inv, col[:, None] * inv], axis=1)
    cos, sin = jnp.cos(ang), jnp.sin(ang)
    zero = jnp.zeros_like(sin)
    lanes = lambda a, b: jnp.stack([a, b], axis=2).reshape(seq, HEAD_DIM)
    return lanes(cos, cos), lanes(-sin, zero), lanes(zero, sin)


def _permute_w_in(w_in):
    sizes = (GLA_KEY, GLA_KEY, GLA_VAL, GLA_VAL, GLA_RANK, GLA_RANK,
             ATTN_Q, ATTN_KV, ATTN_KV, D_MODEL, D_MODEL)
    offs = np.cumsum((0,) + sizes)
    gq, gk, gv, gg, ra_f, ra_b, aq, ak, av, gate_a, gate_b = (
        w_in[:, :, offs[n]:offs[n + 1]] for n in range(len(sizes)))
    main = jnp.concatenate([gv, gg, aq, gate_a, gate_b, gq, gk, ak, av], axis=2)
    assert main.shape[2] == PROJ_WIDTH
    rank = jnp.concatenate([ra_f, ra_b], axis=2)
    rank = jnp.pad(rank, ((0, 0), (0, 0), (0, RANK_PAD - 2 * GLA_RANK)))
    return main.astype(BF16), rank.astype(BF16)


def _pad_w_alpha(w_alpha_up):
    pads = [((0, 0), (d * GLA_RANK, RANK_PAD - (d + 1) * GLA_RANK), (0, 0)) for d in range(2)]
    return jnp.stack([jnp.pad(w_alpha_up[:, d], pads[d]) for d in range(2)], axis=1)


def kernel(x, p, g_mix_pre, w_in, w_alpha_up, b_alpha, g_gla_out, g_q_norm, g_k_norm, w_o_gla,
           w_o_attn, w_out, g_mix_post, g_ffn_pre, w_ffn_in, w_ffn_out, g_ffn_post, w_ple_proj,
           w_ple_gate, g_ple_post):
    batch, seq, _ = x.shape
    depth = p.shape[0]
    t = batch * seq
    cos_t, sa_t, sb_t = _rope_tables(seq)
    gain = lambda g: g.reshape(depth, 1, -1)
    w_main, w_rank = _permute_w_in(w_in)
    wup_pad = _pad_w_alpha(w_alpha_up)
    b_alpha = b_alpha.reshape(depth, 2, 1, GLA_KEY)
    w_o_gla, w_o_attn, w_out, w_ffn_in, w_ffn_out, w_ple_proj, w_ple_gate = (
        w.astype(BF16) for w in (w_o_gla, w_o_attn, w_out, w_ffn_in, w_ffn_out, w_ple_proj, w_ple_gate))
    h = x.reshape(t, D_MODEL)
    for i in range(depth):
        proj, rank = _proj(i, h, gain(g_mix_pre), w_main, w_rank, cos_t, sa_t, sb_t, gain(g_k_norm),
                           seq=seq)
        gla_out = _gla(i, proj, rank, wup_pad, b_alpha, gain(g_gla_out), batch=batch, seq=seq)
        attn_out = _attn(i, proj, cos_t, sa_t, sb_t, gain(g_q_norm), gain(g_k_norm),
                         batch=batch, seq=seq)
        h = _merge(i, gla_out, attn_out, proj, h, w_o_gla, w_o_attn, w_out, gain(g_mix_post))
        h = _ffn(i, h, p, gain(g_ffn_pre), w_ffn_in, w_ffn_out, gain(g_ffn_post), w_ple_proj,
                 w_ple_gate, gain(g_ple_post), seq=seq)
    return h.reshape(batch, seq, D_MODEL)
```
